```python
import math
import jax, jax.numpy as jnp
from jax import lax
import numpy as np

D_MODEL = 2048
BATCH = 8
SEQ = 2048
DEPTH = 1

GRID_W = 64
DIFF_HEADS = 8
DIFF_QK_DIM = 64
DIFF_V_DIM = 2 * DIFF_QK_DIM
NA_HEADS = 8
NA_DIM = 128
NA_WIN_ROWS = 8
NA_WIN_COLS = 16
N_GROUPS = 4
EXPERTS_PER_GROUP = 8
TOP_K_IN_GROUP = 2
EXPERT_FF = 1024
Q_BLOCK = 128
EPS = 1e-6

DIFF_QK_W = DIFF_HEADS * 2 * DIFF_QK_DIM
DIFF_V_W = DIFF_HEADS * DIFF_V_DIM
NA_W = NA_HEADS * NA_DIM
IN_SPLITS = [DIFF_QK_W,
             2 * DIFF_QK_W,
             2 * DIFF_QK_W + DIFF_V_W,
             2 * DIFF_QK_W + DIFF_V_W + NA_W,
             2 * DIFF_QK_W + DIFF_V_W + 2 * NA_W,
             2 * DIFF_QK_W + DIFF_V_W + 3 * NA_W,
             2 * DIFF_QK_W + DIFF_V_W + 3 * NA_W + D_MODEL]
IN_W = 2 * DIFF_QK_W + DIFF_V_W + 3 * NA_W + 2 * D_MODEL

kernel_name = "hybrid_diffattn_natten_hmoe_encoder"


def rms_norm(x, g):
    xf = x.astype(jnp.float32)
    y = xf * lax.rsqrt(jnp.mean(xf * xf, axis=-1, keepdims=True) + EPS)
    return (y * g.astype(jnp.float32)).astype(x.dtype)


def alibi_slopes(n_heads):
    return 2.0 ** (-8.0 * jnp.arange(1, n_heads + 1, dtype=jnp.float32) / n_heads)


def diff_attention(q, k, v, q_norm, k_norm, lam, subln, lambda_init):
    B, S, H, _, dk = q.shape
    dv = v.shape[-1]
    q = rms_norm(q, q_norm) * (dk ** -0.5)
    k = rms_norm(k, k_norm)
    lam = lam.astype(jnp.float32)
    lam_full = (jnp.exp(jnp.sum(lam[0] * lam[1])) - jnp.exp(jnp.sum(lam[2] * lam[3]))
                + lambda_init)
    slopes = alibi_slopes(H)
    kpos = jnp.arange(S)
    nb = S // Q_BLOCK
    qb = q.reshape(B, nb, Q_BLOCK, H, 2, dk).transpose(1, 0, 2, 3, 4, 5)
    starts = jnp.arange(nb) * Q_BLOCK

    def block(args):
        q_blk, start = args
        s = jnp.einsum('bqhmd,bkhmd->bhmqk', q_blk, k,
                       preferred_element_type=jnp.float32)
        qpos = start + jnp.arange(Q_BLOCK)
        dist = jnp.abs(qpos[:, None] - kpos[None, :]).astype(jnp.float32)
        s = s - slopes[:, None, None, None] * dist[None, None]
        p = jax.nn.softmax(s, axis=-1)
        p = p[:, :, 0] - lam_full * p[:, :, 1]
        return jnp.einsum('bhqk,bkhd->bqhd', p.astype(v.dtype), v)

    o = lax.map(block, (qb, starts))
    o = o.transpose(1, 0, 2, 3, 4).reshape(B, S, H, dv)
    o = rms_norm(o, subln) * (1.0 - lambda_init)
    return o.reshape(B, S, H * dv)


def neighbourhood_attention(q, k, v, q_norm, k_norm, rpb):
    B, S, H, d = q.shape
    rows = S // GRID_W
    wr = min(NA_WIN_ROWS, rows)
    q = rms_norm(q, q_norm) * (d ** -0.5)
    k = rms_norm(k, k_norm)
    qg = q.reshape(B, rows, GRID_W, H, d)
    kg = k.reshape(B, rows, GRID_W, H, d)
    vg = v.reshape(B, rows, GRID_W, H, d)
    r = jnp.arange(rows)
    row_start = jnp.clip(r - wr // 2, 0, rows - wr)
    row_idx = row_start[:, None] + jnp.arange(wr)[None, :]
    k_band = kg[:, row_idx]
    v_band = vg[:, row_idx]
    c = jnp.arange(GRID_W)
    col_start = jnp.clip(c - NA_WIN_COLS // 2, 0, GRID_W - NA_WIN_COLS)
    col_in = ((c[None, :] >= col_start[:, None])
              & (c[None, :] < col_start[:, None] + NA_WIN_COLS))
    dc = jnp.clip(c[None, :] - c[:, None], -(NA_WIN_COLS - 1), NA_WIN_COLS - 1)
    dr = row_idx - r[:, None]
    bias = rpb[:, dr[:, None, :, None] + (NA_WIN_ROWS - 1),
               dc[None, :, None, :] + (NA_WIN_COLS - 1)]
    s = jnp.einsum('brchd,brwkhd->bhrcwk', qg, k_band,
                   preferred_element_type=jnp.float32)
    s = s + bias.astype(jnp.float32)[None]
    s = jnp.where(col_in[None, None, None, :, None, :], s, -jnp.inf)
    p = jax.nn.softmax(s.reshape(B, H, rows, GRID_W, wr * GRID_W), axis=-1)
    p = p.reshape(B, H, rows, GRID_W, wr, GRID_W).astype(v.dtype)
    o = jnp.einsum('bhrcwk,brwkhd->brchd', p, v_band)
    return o.reshape(B, S, H * d)


def hierarchical_moe(h, w_rg, b_rg, w_re, b_re, w_gate, w_up, w_down):
    B, S, D = h.shape
    t = h.reshape(B * S, D)
    g_logits = jnp.dot(t, w_rg).astype(jnp.float32) + b_rg.astype(jnp.float32)
    g_prob = jax.nn.softmax(g_logits, axis=-1)
    g_sel = jnp.argmax(g_logits, axis=-1)
    g_w = jnp.take_along_axis(g_prob, g_sel[:, None], axis=-1)
    e_logits = (jnp.dot(t, w_re).astype(jnp.float32).reshape(-1, N_GROUPS, EXPERTS_PER_GROUP)
                + b_re.astype(jnp.float32))
    e_logits = jnp.take_along_axis(e_logits, g_sel[:, None, None], axis=1)[:, 0]
    e_prob = jax.nn.softmax(e_logits, axis=-1)
    top_w, top_i = lax.top_k(e_prob, TOP_K_IN_GROUP)
    top_w = top_w / jnp.sum(top_w, axis=-1, keepdims=True) * g_w
    e_comb = jnp.sum(jax.nn.one_hot(top_i, EXPERTS_PER_GROUP, dtype=jnp.float32)
                     * top_w[..., None], axis=1)
    combine = jax.nn.one_hot(g_sel, N_GROUPS, dtype=jnp.float32)[:, :, None] * e_comb[:, None, :]
    y = jnp.zeros((B * S, D), jnp.float32)
    for g in range(N_GROUPS):
        for e in range(EXPERTS_PER_GROUP):
            hid = jax.nn.silu(jnp.dot(t, w_gate[g, e])) * jnp.dot(t, w_up[g, e])
            y = y + jnp.dot(hid * combine[:, g, e, None].astype(hid.dtype), w_down[g, e]).astype(jnp.float32)
    return y.reshape(B, S, D).astype(h.dtype)


def setup_inputs(seed: int = 0) -> dict:
    key = jax.random.key(seed)
    ks = jax.random.split(key, 24)
    L, D = DEPTH, D_MODEL
    G, E, F = N_GROUPS, EXPERTS_PER_GROUP, EXPERT_FF
    nrm = lambda k, shape, s: jax.random.normal(k, shape, jnp.float32) * s
    gain = lambda k, shape: 1.0 + 0.01 * jax.random.normal(k, shape, jnp.float32)
    return {
        "x": nrm(ks[0], (BATCH, SEQ, D), 1.0),
        "mix_norm": gain(ks[1], (L, D)),
        "w_in": nrm(ks[2], (L, D, IN_W), D ** -0.5),
        "diff_q_norm": gain(ks[3], (L, DIFF_QK_DIM)),
        "diff_k_norm": gain(ks[4], (L, DIFF_QK_DIM)),
        "diff_lambda": nrm(ks[5], (L, 4, DIFF_QK_DIM), 0.1),
        "diff_subln": gain(ks[6], (L, DIFF_V_DIM)),
        "na_q_norm": gain(ks[7], (L, NA_DIM)),
        "na_k_norm": gain(ks[8], (L, NA_DIM)),
        "na_rpb": nrm(ks[9], (L, NA_HEADS, 2 * NA_WIN_ROWS - 1, 2 * NA_WIN_COLS - 1), 0.05),
        "w_diff_out": nrm(ks[10], (L, DIFF_V_W, D), DIFF_V_W ** -0.5),
        "w_na_out": nrm(ks[11], (L, NA_W, D), NA_W ** -0.5),
        "w_out": nrm(ks[12], (L, D, D), D ** -0.5),
        "ffn_norm": gain(ks[13], (L, D)),
        "w_router_group": nrm(ks[14], (L, D, G), D ** -0.5),
        "b_router_group": nrm(ks[15], (L, G), 0.01),
        "w_router_expert": nrm(ks[16], (L, D, G * E), D ** -0.5),
        "b_router_expert": nrm(ks[17], (L, G, E), 0.01),
        "w_expert_gate": nrm(ks[18], (L, G, E, D, F), D ** -0.5),
        "w_expert_up": nrm(ks[19], (L, G, E, D, F), D ** -0.5),
        "w_expert_down": nrm(ks[20], (L, G, E, F, D), F ** -0.5),
    }


def reference(x, mix_norm, w_in, diff_q_norm, diff_k_norm, diff_lambda, diff_subln,
              na_q_norm, na_k_norm, na_rpb, w_diff_out, w_na_out, w_out, ffn_norm,
              w_router_group, b_router_group, w_router_expert, b_router_expert,
              w_expert_gate, w_expert_up, w_expert_down):
    B, S, D = x.shape
    for l in range(DEPTH):
        lambda_init = 0.8 - 0.6 * math.exp(-0.3 * l)
        h = rms_norm(x, mix_norm[l])
        proj = jnp.dot(h, w_in[l])
        dq, dk, dv, nq, nk, nv, ga, gb = jnp.split(proj, IN_SPLITS, axis=-1)
        ya = diff_attention(dq.reshape(B, S, DIFF_HEADS, 2, DIFF_QK_DIM),
                            dk.reshape(B, S, DIFF_HEADS, 2, DIFF_QK_DIM),
                            dv.reshape(B, S, DIFF_HEADS, DIFF_V_DIM),
                            diff_q_norm[l], diff_k_norm[l], diff_lambda[l],
                            diff_subln[l], lambda_init)
        yb = neighbourhood_attention(nq.reshape(B, S, NA_HEADS, NA_DIM),
                                     nk.reshape(B, S, NA_HEADS, NA_DIM),
                                     nv.reshape(B, S, NA_HEADS, NA_DIM),
                                     na_q_norm[l], na_k_norm[l], na_rpb[l])
        ya = jnp.dot(ya, w_diff_out[l])
        yb = jnp.dot(yb, w_na_out[l])
        merged = jax.nn.sigmoid(ga) * ya + jax.nn.sigmoid(gb) * yb
        x = x + jnp.dot(merged, w_out[l])
        h = rms_norm(x, ffn_norm[l])
        x = x + hierarchical_moe(h, w_router_group[l], b_router_group[l],
                                 w_router_expert[l], b_router_expert[l],
                                 w_expert_gate[l], w_expert_up[l], w_expert_down[l])
    return x
```

```python
import functools
import math

import jax
import jax.numpy as jnp
from jax import lax
from jax.experimental import pallas as pl
from jax.experimental.pallas import tpu as pltpu

F32 = jnp.float32
BF16 = jnp.bfloat16
I32 = jnp.int32

EPS = 1e-6
GRID_W = 64
DIFF_HEADS = 8
DIFF_QK_DIM = 64
NA_HEADS = 8
NA_DIM = 128
NA_WIN_ROWS = 8
NA_WIN_COLS = 16
TOP_K = 2
HEAD_W = 128
DQ_BLK, DK_BLK, DV_BLK = 0, DIFF_HEADS, 2 * DIFF_HEADS
NQ_BLK = 3 * DIFF_HEADS
NK_BLK = NQ_BLK + NA_HEADS
NV_BLK = NQ_BLK + 2 * NA_HEADS
QKV_W = (3 * DIFF_HEADS + 3 * NA_HEADS) * HEAD_W

ROUTER_ROWS = 40
VMEM_LIMIT = 56 * 1024 * 1024

_NT = (((1,), (1,)), ((), ()))


def _cparams(sem, vmem=VMEM_LIMIT):
    return pltpu.CompilerParams(dimension_semantics=sem, vmem_limit_bytes=vmem)


def _inproj_body(x_ref, g_ref, w_ref, o_ref, h_scr, *, rows):
    @pl.when(pl.program_id(1) == 0)
    def _():
        def chunk(c, carry):
            r = pl.ds(pl.multiple_of(c * rows, rows), rows)
            x = x_ref[r, :]
            ms = jnp.mean(x * x, axis=-1, keepdims=True)
            h_scr[r, :] = (x * lax.rsqrt(ms + EPS) * g_ref[...]).astype(BF16)
            return carry
        lax.fori_loop(0, x_ref.shape[0] // rows, chunk, 0)

    o_ref[...] = jnp.dot(h_scr[...], w_ref[...], preferred_element_type=F32).astype(BF16)


def _inproj(x2, gain, w_bf16, *, tm, tn):
    T, D = x2.shape
    N = w_bf16.shape[1]
    return pl.pallas_call(
        functools.partial(_inproj_body, rows=128),
        grid=(T // tm, N // tn),
        in_specs=[pl.BlockSpec((tm, D), lambda i, j: (i, 0)),
                  pl.BlockSpec((1, D), lambda i, j: (0, 0)),
                  pl.BlockSpec((D, tn), lambda i, j: (0, j))],
        out_specs=pl.BlockSpec((tm, tn), lambda i, j: (i, j)),
        out_shape=jax.ShapeDtypeStruct((T, N), BF16),
        scratch_shapes=[pltpu.VMEM((tm, D), BF16)],
        compiler_params=_cparams(("parallel", "arbitrary")),
        name="inproj",
    )(x2, gain.reshape(1, D), w_bf16)


def _half_rms(x, gain, first):
    x2 = x * x
    s0 = jnp.sum(jnp.where(first, x2, 0.0), axis=-1, keepdims=True)
    s1 = jnp.sum(jnp.where(first, 0.0, x2), axis=-1, keepdims=True)
    inv = jnp.where(first, lax.rsqrt(s0 * (1.0 / DIFF_QK_DIM) + EPS),
                    lax.rsqrt(s1 * (1.0 / DIFF_QK_DIM) + EPS))
    return x * inv * gain


def _diff_body(slope_ref, lam_ref, q_ref, k_ref, v_ref, qg_ref, kg_ref, sg_ref, o_ref, kn_scr,
               *, tq, seq, out_scale):
    h = pl.program_id(1)
    qi = pl.program_id(2)
    lane = lax.broadcasted_iota(I32, (1, HEAD_W), 1)
    first = lane < DIFF_QK_DIM

    @pl.when(qi == 0)
    def _():
        kn_scr[...] = _half_rms(k_ref[...].astype(F32), kg_ref[...], first).astype(BF16)

    q = _half_rms(q_ref[...].astype(F32), qg_ref[...], first) * (DIFF_QK_DIM ** -0.5)
    q0 = jnp.where(first, q, 0.0).astype(BF16)
    q1 = jnp.where(first, 0.0, q).astype(BF16)
    kn = kn_scr[...]
    row = qi * tq + lax.broadcasted_iota(I32, (tq, seq), 0)
    col = lax.broadcasted_iota(I32, (tq, seq), 1)
    bias = jnp.abs(row - col).astype(F32) * (-slope_ref[h])
    v = v_ref[...]

    def one_map(qm):
        s = lax.dot_general(qm, kn, _NT, preferred_element_type=F32) + bias
        m = jnp.max(s, axis=-1, keepdims=True)
        p = jnp.exp(s - m)
        l = jnp.sum(p, axis=-1, keepdims=True)
        o = jnp.dot(p.astype(BF16), v, preferred_element_type=F32)
        return o / l

    o = one_map(q0) - lam_ref[0] * one_map(q1)
    ms = jnp.mean(o * o, axis=-1, keepdims=True)
    o_ref[...] = (o * lax.rsqrt(ms + EPS) * sg_ref[...] * out_scale).astype(BF16)


def _diff_attention(proj, slopes, lam, q_gain, k_gain, sub_gain, *, batch, seq, tq, out_scale):
    T = proj.shape[0]
    nq = seq // tq
    smem = pl.BlockSpec(memory_space=pltpu.SMEM)
    vec = pl.BlockSpec((1, HEAD_W), lambda b, h, i: (0, 0))
    return pl.pallas_call(
        functools.partial(_diff_body, tq=tq, seq=seq, out_scale=out_scale),
        grid=(batch, DIFF_HEADS, nq),
        in_specs=[smem, smem,
                  pl.BlockSpec((tq, HEAD_W), lambda b, h, i: (b * nq + i, DQ_BLK + h)),
                  pl.BlockSpec((seq, HEAD_W), lambda b, h, i: (b, DK_BLK + h)),
                  pl.BlockSpec((seq, HEAD_W), lambda b, h, i: (b, DV_BLK + h)),
                  vec, vec, vec],
        out_specs=pl.BlockSpec((tq, HEAD_W), lambda b, h, i: (b * nq + i, h)),
        out_shape=jax.ShapeDtypeStruct((T, DIFF_HEADS * HEAD_W), BF16),
        scratch_shapes=[pltpu.VMEM((seq, HEAD_W), BF16)],
        compiler_params=_cparams(("parallel", "parallel", "arbitrary")),
        name="diff_attn",
    )(slopes, lam, proj, proj, proj, q_gain, k_gain, sub_gain)


def _na_bias_table(rpb, rows):
    wr = min(NA_WIN_ROWS, rows)
    c = jnp.arange(GRID_W)
    col_start = jnp.clip(c - NA_WIN_COLS // 2, 0, GRID_W - NA_WIN_COLS)
    col_in = (c[None, :] >= col_start[:, None]) & (c[None, :] < col_start[:, None] + NA_WIN_COLS)
    dc = jnp.clip(c[None, :] - c[:, None], -(NA_WIN_COLS - 1), NA_WIN_COLS - 1) + (NA_WIN_COLS - 1)
    pat = jnp.arange(wr)
    dr = jnp.arange(wr)[None, :] - pat[:, None] + (NA_WIN_ROWS - 1)
    b = rpb[:, dr[:, None, :, None], dc[None, :, None, :]]
    b = jnp.where(col_in[None, None, :, None, :], b.astype(F32), -jnp.inf)
    return b.reshape(rpb.shape[0], wr, GRID_W, wr * GRID_W)


def _na_body(q_ref, k_ref, v_ref, qg_ref, kg_ref, bias_ref, o_ref, qn_scr, kn_scr, *, rows, wr):
    def rms(x, g):
        return x * lax.rsqrt(jnp.mean(x * x, axis=-1, keepdims=True) + EPS) * g

    qn_scr[...] = (rms(q_ref[...].astype(F32), qg_ref[...]) * (NA_DIM ** -0.5)).astype(BF16)
    kn_scr[...] = rms(k_ref[...].astype(F32), kg_ref[...]).astype(BF16)
    win = wr * GRID_W

    def one_row(r, start_row, pattern):
        qs = pl.ds(pl.multiple_of(r * GRID_W, GRID_W), GRID_W)
        ks = pl.ds(pl.multiple_of(start_row * GRID_W, GRID_W), win)
        s = lax.dot_general(qn_scr[qs, :], kn_scr[ks, :], _NT, preferred_element_type=F32)
        s = s + bias_ref[0, pattern]
        m = jnp.max(s, axis=-1, keepdims=True)
        p = jnp.exp(s - m)
        l = jnp.sum(p, axis=-1, keepdims=True)
        o = jnp.dot(p.astype(BF16), v_ref[ks, :], preferred_element_type=F32)
        o_ref[qs, :] = (o / l).astype(BF16)

    half = wr // 2
    lo_edge = min(half, rows)
    hi_edge = max(rows - (wr - half) + 1, lo_edge)
    for r in range(lo_edge):
        one_row(r, 0, r)
    for r in range(hi_edge, rows):
        one_row(r, rows - wr, r - (rows - wr))

    def mid(r, carry):
        one_row(r, r - half, half)
        return carry
    lax.fori_loop(lo_edge, hi_edge, mid, 0)


def _na_attention(proj, q_gain, k_gain, bias_tab, *, batch, seq):
    T = proj.shape[0]
    rows = seq // GRID_W
    wr = min(NA_WIN_ROWS, rows)
    vec = pl.BlockSpec((1, NA_DIM), lambda h, b: (0, 0))
    return pl.pallas_call(
        functools.partial(_na_body, rows=rows, wr=wr),
        grid=(NA_HEADS, batch),
        in_specs=[pl.BlockSpec((seq, HEAD_W), lambda h, b: (b, NQ_BLK + h)),
                  pl.BlockSpec((seq, HEAD_W), lambda h, b: (b, NK_BLK + h)),
                  pl.BlockSpec((seq, HEAD_W), lambda h, b: (b, NV_BLK + h)),
                  vec, vec,
                  pl.BlockSpec((1, wr, GRID_W, wr * GRID_W), lambda h, b: (h, 0, 0, 0))],
        out_specs=pl.BlockSpec((seq, HEAD_W), lambda h, b: (b, h)),
        out_shape=jax.ShapeDtypeStruct((T, NA_HEADS * NA_DIM), BF16),
        scratch_shapes=[pltpu.VMEM((seq, HEAD_W), BF16), pltpu.VMEM((seq, HEAD_W), BF16)],
        compiler_params=_cparams(("parallel", "parallel")),
        name="na_attn",
    )(proj, proj, proj, q_gain, k_gain, bias_tab)


def _outproj_body(oa_ref, ob_ref, ga_ref, gb_ref, x_ref, wd_ref, wn_ref, wo_ref, fg_ref, wr_ref,
                  x1_ref, h2_ref, lt_ref):
    ya = jnp.dot(oa_ref[...], wd_ref[...], preferred_element_type=F32)
    yb = jnp.dot(ob_ref[...], wn_ref[...], preferred_element_type=F32)
    merged = (jax.nn.sigmoid(ga_ref[...].astype(F32)) * ya
              + jax.nn.sigmoid(gb_ref[...].astype(F32)) * yb)
    x1 = x_ref[...] + jnp.dot(merged.astype(BF16), wo_ref[...], preferred_element_type=F32)
    x1_ref[...] = x1
    h2 = x1 * lax.rsqrt(jnp.mean(x1 * x1, axis=-1, keepdims=True) + EPS) * fg_ref[...]
    h2_ref[...] = h2
    lt_ref[...] = lax.dot_general(wr_ref[...], h2, _NT, precision=lax.Precision.HIGHEST,
                                  preferred_element_type=F32)


def _outproj(oa, ob, proj, x2, wd, wn, wo, ffn_gain, wr_t, *, tm):
    T, D = x2.shape
    gate_blk = QKV_W // D
    const = lambda shape: pl.BlockSpec(shape, lambda i: (0, 0), pipeline_mode=pl.Buffered(1))
    return pl.pallas_call(
        _outproj_body,
        grid=(T // tm,),
        in_specs=[pl.BlockSpec((tm, oa.shape[1]), lambda i: (i, 0)),
                  pl.BlockSpec((tm, ob.shape[1]), lambda i: (i, 0)),
                  pl.BlockSpec((tm, D), lambda i: (i, gate_blk)),
                  pl.BlockSpec((tm, D), lambda i: (i, gate_blk + 1)),
                  pl.BlockSpec((tm, D), lambda i: (i, 0)),
                  const(wd.shape), const(wn.shape), const(wo.shape),
                  const((1, D)), const(wr_t.shape)],
        out_specs=[pl.BlockSpec((tm, D), lambda i: (i, 0)),
                   pl.BlockSpec((tm, D), lambda i: (i, 0)),
                   pl.BlockSpec((ROUTER_ROWS, tm), lambda i: (0, i))],
        out_shape=[jax.ShapeDtypeStruct((T, D), F32),
                   jax.ShapeDtypeStruct((T, D), F32),
                   jax.ShapeDtypeStruct((ROUTER_ROWS, T), F32)],
        compiler_params=_cparams(("parallel",)),
        name="outproj",
    )(oa, ob, proj, proj, x2, wd, wn, wo, ffn_gain.reshape(1, D), wr_t)


def _route_body(lt_ref, bias_ref, pos_ref, wts_ref, meta_ref, cnt_scr, carry_scr,
                *, n_groups, n_exp, tm, tc):
    ne = n_groups * n_exp
    phase = pl.program_id(0)
    step = pl.program_id(1)

    l = lt_ref[...] + bias_ref[...]
    el = l[0:ne]
    gl = l[ne:ne + n_groups]
    gmax = jnp.max(gl, axis=0, keepdims=True)
    gi = lax.broadcasted_iota(I32, gl.shape, 0)
    gsel = jnp.min(jnp.where(gl == gmax, gi, n_groups), axis=0, keepdims=True)
    gw = 1.0 / jnp.sum(jnp.exp(gl - gmax), axis=0, keepdims=True)
    ei = lax.broadcasted_iota(I32, el.shape, 0)
    lo = gsel * n_exp
    elm = jnp.where((ei >= lo) & (ei < lo + n_exp), el, -jnp.inf)
    m1 = jnp.max(elm, axis=0, keepdims=True)
    i1 = jnp.min(jnp.where(elm == m1, ei, ne), axis=0, keepdims=True)
    elm2 = jnp.where(ei == i1, -jnp.inf, elm)
    m2 = jnp.max(elm2, axis=0, keepdims=True)
    i2 = jnp.min(jnp.where(elm2 == m2, ei, ne), axis=0, keepdims=True)
    r = jnp.exp(m2 - m1)
    w1 = gw / (1.0 + r)
    w2 = w1 * r
    oh1 = ei == i1
    oh2 = ei == i2
    both = jnp.where(oh1 | oh2, 1.0, 0.0)

    @pl.when((phase == 0) & (step == 0))
    def _():
        cnt_scr[...] = jnp.zeros_like(cnt_scr)
        carry_scr[...] = jnp.zeros_like(carry_scr)

    @pl.when(phase == 0)
    def _():
        cnt_scr[...] += jnp.sum(both, axis=1, keepdims=True)

    @pl.when(phase == 1)
    def _():
        cnt = cnt_scr[...]
        tiles = jnp.floor((cnt + (tm - 1)) * (1.0 / tm))
        ea = lax.broadcasted_iota(I32, (ne, ne), 0)
        eb = lax.broadcasted_iota(I32, (ne, ne), 1)
        lower = jnp.where(eb < ea, 1.0, 0.0).astype(BF16)
        tiles_b = jnp.broadcast_to(tiles, (ne, 128)).astype(BF16)
        start_t = jnp.dot(lower, tiles_b, preferred_element_type=F32)[:, 0:1]
        ta = lax.broadcasted_iota(I32, (tc, tc), 0)
        tb = lax.broadcasted_iota(I32, (tc, tc), 1)
        upper = jnp.where(ta < tb, 1.0, 0.0).astype(BF16)
        prefix = jnp.dot(both.astype(BF16), upper, preferred_element_type=F32)
        base = prefix + carry_scr[...] + start_t * tm
        pos1 = jnp.sum(jnp.where(oh1, base, 0.0), axis=0, keepdims=True)
        pos2 = jnp.sum(jnp.where(oh2, base, 0.0), axis=0, keepdims=True)
        carry_scr[...] += jnp.sum(both, axis=1, keepdims=True)
        pos_ref[...] = jnp.zeros_like(pos_ref)
        pos_ref[0:1, :] = pos1.astype(I32)
        pos_ref[1:2, :] = pos2.astype(I32)
        wts_ref[...] = jnp.zeros_like(wts_ref)
        wts_ref[0:1, :] = w1
        wts_ref[1:2, :] = w2

        nt = meta_ref.shape[1]
        tj = lax.broadcasted_iota(I32, (ne, nt), 1).astype(F32)
        end_t = start_t + tiles
        texp = jnp.sum(jnp.where(end_t <= tj, 1.0, 0.0), axis=0, keepdims=True)
        is_first = jnp.sum(jnp.where((start_t == tj) & (tiles > 0.0), 1.0, 0.0),
                           axis=0, keepdims=True)
        is_last = jnp.sum(jnp.where((end_t - 1.0 == tj) & (tiles > 0.0), 1.0, 0.0),
                          axis=0, keepdims=True)
        nused = jnp.broadcast_to(jnp.sum(tiles, axis=0, keepdims=True), (1, nt))
        meta_ref[...] = jnp.zeros_like(meta_ref)
        meta_ref[0:1, :] = jnp.minimum(texp, ne - 1.0).astype(I32)
        meta_ref[1:2, :] = is_first.astype(I32)
        fill = is_last + jnp.where(tj[0:1, :] >= nused, 1.0, 0.0)
        meta_ref[2:3, :] = fill.astype(I32)
        meta_ref[3:4, :] = nused.astype(I32)


def _route(lt, bias_col, *, n_groups, n_exp, tm, n_tiles, tc):
    T = lt.shape[1]
    ne = n_groups * n_exp
    nt_pad = -(-n_tiles // 128) * 128
    return pl.pallas_call(
        functools.partial(_route_body, n_groups=n_groups, n_exp=n_exp, tm=tm, tc=tc),
        grid=(2, T // tc),
        in_specs=[pl.BlockSpec((ROUTER_ROWS, tc), lambda p, i: (0, i)),
                  pl.BlockSpec((ROUTER_ROWS, 1), lambda p, i: (0, 0))],
        out_specs=[pl.BlockSpec((8, tc), lambda p, i: (0, i * p)),
                   pl.BlockSpec((8, tc), lambda p, i: (0, i * p)),
                   pl.BlockSpec((8, nt_pad), lambda p, i: (0, 0))],
        out_shape=[jax.ShapeDtypeStruct((8, T), I32),
                   jax.ShapeDtypeStruct((8, T), F32),
                   jax.ShapeDtypeStruct((8, nt_pad), I32)],
        scratch_shapes=[pltpu.VMEM((ne, 1), F32), pltpu.VMEM((ne, 1), F32)],
        compiler_params=_cparams(("arbitrary", "arbitrary")),
        name="route",
    )(lt, bias_col)


def _dispatch_body(fill_ref, pos1_ref, pos2_ref, h_ref, xs_ref, zero_scr, sem, zsem, *, td, tm):
    n_tiles = fill_ref.shape[0]

    def zero_copy(j):
        return pltpu.make_async_copy(zero_scr, xs_ref.at[pl.ds(pl.multiple_of(j * tm, tm), tm), :], zsem)

    @pl.when(pl.program_id(0) == 0)
    def _():
        zero_scr[...] = jnp.zeros_like(zero_scr)

        def start(j, carry):
            @pl.when(fill_ref[j] == 1)
            def _():
                zero_copy(j).start()
            return carry
        lax.fori_loop(0, n_tiles, start, 0)

        def wait(j, carry):
            @pl.when(fill_ref[j] == 1)
            def _():
                zero_copy(j).wait()
            return carry
        lax.fori_loop(0, n_tiles, wait, 0)

    def row_copy(t, pos):
        return pltpu.make_async_copy(h_ref.at[pl.ds(t, 1), :], xs_ref.at[pl.ds(pos, 1), :], sem)

    def issue(t, carry):
        row_copy(t, pos1_ref[0, 0, t]).start()
        row_copy(t, pos2_ref[0, 0, t]).start()
        return carry
    lax.fori_loop(0, td, issue, 0)

    def drain(t, carry):
        row_copy(t, 0).wait()
        row_copy(t, 0).wait()
        return carry
    lax.fori_loop(0, td, drain, 0)


def _dispatch(fill, pos1, pos2, h2, *, n_rows, td, tm):
    T, D = h2.shape
    nblk = T // td
    smem3 = pl.BlockSpec((1, 1, td), lambda i, last: (i, 0, 0), memory_space=pltpu.SMEM)
    grid_spec = pltpu.PrefetchScalarGridSpec(
        num_scalar_prefetch=1,
        grid=(nblk,),
        in_specs=[smem3, smem3, pl.BlockSpec((td, D), lambda i, last: (i, 0))],
        out_specs=pl.BlockSpec(memory_space=pl.ANY),
        scratch_shapes=[pltpu.VMEM((tm, D), F32), pltpu.SemaphoreType.DMA(()),
                        pltpu.SemaphoreType.DMA(())],
    )
    return pl.pallas_call(
        functools.partial(_dispatch_body, td=td, tm=tm),
        grid_spec=grid_spec,
        out_shape=jax.ShapeDtypeStruct((n_rows, D), F32),
        compiler_params=_cparams(("arbitrary",)),
        name="dispatch",
    )(fill, pos1.reshape(nblk, 1, td), pos2.reshape(nblk, 1, td), h2)


def _gateup_body(texp_ref, first_ref, nused_ref, x_ref, wg_ref, wu_ref, hid_ref, wg_scr, wu_scr):
    j = pl.program_id(1)

    @pl.when(j < nused_ref[0])
    def _():
        @pl.when(first_ref[j] == 1)
        def _():
            wg_scr[...] = wg_ref[0].astype(BF16)
            wu_scr[...] = wu_ref[0].astype(BF16)
        x = x_ref[...].astype(BF16)
        g = jnp.dot(x, wg_scr[...], preferred_element_type=F32)
        u = jnp.dot(x, wu_scr[...], preferred_element_type=F32)
        hid_ref[...] = (g * jax.nn.sigmoid(g) * u).astype(BF16)

    @pl.when(j >= nused_ref[0])
    def _():
        hid_ref[...] = jnp.zeros_like(hid_ref)


def _gateup(texp, first, nused, xs, wg, wu, *, tm, tf):
    P, D = xs.shape
    F = wg.shape[2]
    n_tiles = P // tm

    def tile(j, nused):
        return jnp.minimum(j, nused[0] - 1)

    grid_spec = pltpu.PrefetchScalarGridSpec(
        num_scalar_prefetch=3,
        grid=(F // tf, n_tiles),
        in_specs=[pl.BlockSpec((tm, D), lambda c, j, te, fi, nu: (tile(j, nu), 0)),
                  pl.BlockSpec((1, D, tf), lambda c, j, te, fi, nu: (te[tile(j, nu)], 0, c)),
                  pl.BlockSpec((1, D, tf), lambda c, j, te, fi, nu: (te[tile(j, nu)], 0, c))],
        out_specs=pl.BlockSpec((tm, tf), lambda c, j, te, fi, nu: (j, c)),
        scratch_shapes=[pltpu.VMEM((D, tf), BF16), pltpu.VMEM((D, tf), BF16)],
    )
    return pl.pallas_call(
        _gateup_body,
        grid_spec=grid_spec,
        out_shape=jax.ShapeDtypeStruct((P, F), BF16),
        compiler_params=_cparams(("arbitrary", "arbitrary")),
        name="expert_gateup",
    )(texp, first, nused, xs, wg, wu)


def _down_body(texp_ref, first_ref, nused_ref, hid_ref, wd_ref, y_ref, wd_scr):
    j = pl.program_id(1)

    @pl.when(j < nused_ref[0])
    def _():
        @pl.when(first_ref[j] == 1)
        def _():
            wd_scr[...] = wd_ref[0].astype(BF16)
        y_ref[...] = jnp.dot(hid_ref[...], wd_scr[...], preferred_element_type=F32)

    @pl.when(j >= nused_ref[0])
    def _():
        y_ref[...] = jnp.zeros_like(y_ref)


def _down(texp, first, nused, hid, wd, *, tm, tn):
    P, F = hid.shape
    D = wd.shape[2]
    n_tiles = P // tm

    def tile(j, nused):
        return jnp.minimum(j, nused[0] - 1)

    grid_spec = pltpu.PrefetchScalarGridSpec(
        num_scalar_prefetch=3,
        grid=(D // tn, n_tiles),
        in_specs=[pl.BlockSpec((tm, F), lambda c, j, te, fi, nu: (tile(j, nu), 0)),
                  pl.BlockSpec((1, F, tn), lambda c, j, te, fi, nu: (te[tile(j, nu)], 0, c))],
        out_specs=pl.BlockSpec((tm, tn), lambda c, j, te, fi, nu: (j, c)),
        scratch_shapes=[pltpu.VMEM((F, tn), BF16)],
    )
    return pl.pallas_call(
        _down_body,
        grid_spec=grid_spec,
        out_shape=jax.ShapeDtypeStruct((P, D), F32),
        compiler_params=_cparams(("arbitrary", "arbitrary")),
        name="expert_down",
    )(texp, first, nused, hid, wd)


def _combine_body(pos1_ref, pos2_ref, ys_ref, x1_ref, w1_ref, w2_ref, o_ref, r1_scr, r2_scr, sem, *, tc):
    def row_copy(pos, dst, t):
        return pltpu.make_async_copy(ys_ref.at[pl.ds(pos, 1), :], dst.at[pl.ds(t, 1), :], sem)

    def issue(t, carry):
        row_copy(pos1_ref[0, 0, t], r1_scr, t).start()
        row_copy(pos2_ref[0, 0, t], r2_scr, t).start()
        return carry
    lax.fori_loop(0, tc, issue, 0)

    def drain(t, carry):
        row_copy(0, r1_scr, t).wait()
        row_copy(0, r2_scr, t).wait()
        return carry
    lax.fori_loop(0, tc, drain, 0)

    o_ref[...] = x1_ref[...] + w1_ref[...] * r1_scr[...] + w2_ref[...] * r2_scr[...]


def _combine(pos1, pos2, ys, x1, w1, w2, *, tc):
    T, D = x1.shape
    nblk = T // tc
    smem3 = pl.BlockSpec((1, 1, tc), lambda i: (i, 0, 0), memory_space=pltpu.SMEM)
    col = pl.BlockSpec((tc, 1), lambda i: (i, 0))
    return pl.pallas_call(
        functools.partial(_combine_body, tc=tc),
        grid=(nblk,),
        in_specs=[smem3, smem3, pl.BlockSpec(memory_space=pl.ANY),
                  pl.BlockSpec((tc, D), lambda i: (i, 0)), col, col],
        out_specs=pl.BlockSpec((tc, D), lambda i: (i, 0)),
        out_shape=jax.ShapeDtypeStruct((T, D), F32),
        scratch_shapes=[pltpu.VMEM((tc, D), F32), pltpu.VMEM((tc, D), F32),
                        pltpu.SemaphoreType.DMA(())],
        compiler_params=_cparams(("arbitrary",)),
        name="combine",
    )(pos1.reshape(nblk, 1, tc), pos2.reshape(nblk, 1, tc), ys, x1, w1.reshape(T, 1), w2.reshape(T, 1))


def _tile_sizes(T, D, seq, in_w):
    pick = lambda n, pref: pref if n % pref == 0 else n
    return dict(
        inproj_tm=pick(T, 1024), inproj_tn=next(t for t in (1024, 512, 256, 128) if in_w % t == 0),
        diff_tq=pick(seq, 128),
        outproj_tm=pick(T, 256),
        route_tc=pick(T, 512),
        expert_tm=256, expert_tf=512, expert_tn=pick(D, 1024),
        dispatch_td=pick(T, 256), combine_tc=pick(T, 256),
    )


def _layer(x2, p, l, *, batch, seq):
    T, D = x2.shape
    ts = _tile_sizes(T, D, seq, p["w_in"].shape[-1])
    lambda_init = 0.8 - 0.6 * math.exp(-0.3 * l)
    n_groups, n_exp = p["b_router_expert"].shape[1:]
    ne = n_groups * n_exp
    rows = seq // GRID_W

    proj = _inproj(x2, p["mix_norm"][l], p["w_in"][l].astype(BF16),
                   tm=ts["inproj_tm"], tn=ts["inproj_tn"])

    lam = p["diff_lambda"][l].astype(F32)
    lam_full = (jnp.exp(jnp.sum(lam[0] * lam[1])) - jnp.exp(jnp.sum(lam[2] * lam[3]))
                + lambda_init).reshape(1)
    slopes = 2.0 ** (-8.0 * jnp.arange(1, DIFF_HEADS + 1, dtype=F32) / DIFF_HEADS)
    two = lambda g: jnp.tile(g.astype(F32), 2).reshape(1, HEAD_W)
    oa = _diff_attention(proj, slopes, lam_full, two(p["diff_q_norm"][l]), two(p["diff_k_norm"][l]),
                         p["diff_subln"][l].astype(F32).reshape(1, HEAD_W),
                         batch=batch, seq=seq, tq=ts["diff_tq"], out_scale=1.0 - lambda_init)

    ob = _na_attention(proj, p["na_q_norm"][l].astype(F32).reshape(1, NA_DIM),
                       p["na_k_norm"][l].astype(F32).reshape(1, NA_DIM),
                       _na_bias_table(p["na_rpb"][l], rows), batch=batch, seq=seq)

    wr_t = jnp.concatenate([p["w_router_expert"][l].T, p["w_router_group"][l].T,
                            jnp.zeros((ROUTER_ROWS - ne - n_groups, D), F32)], axis=0).astype(F32)
    b_col = jnp.concatenate([p["b_router_expert"][l].reshape(ne), p["b_router_group"][l],
                             jnp.zeros((ROUTER_ROWS - ne - n_groups,), F32)]).astype(F32).reshape(ROUTER_ROWS, 1)
    x1, h2, lt = _outproj(oa, ob, proj, x2, p["w_diff_out"][l].astype(BF16),
                          p["w_na_out"][l].astype(BF16), p["w_out"][l].astype(BF16),
                          p["ffn_norm"][l].astype(F32), wr_t, tm=ts["outproj_tm"])

    tm = ts["expert_tm"]
    n_tiles = (T * TOP_K) // tm + ne
    pos, wts, meta = _route(lt, b_col, n_groups=n_groups, n_exp=n_exp, tm=tm, n_tiles=n_tiles,
                            tc=ts["route_tc"])
    texp, first, fill, nused = meta[0, :n_tiles], meta[1, :n_tiles], meta[2, :n_tiles], meta[3, :1]

    xs = _dispatch(fill,pos[0], pos[1], h2, n_rows=n_tiles * tm, td=ts["dispatch_td"], tm=tm)
    F = p["w_expert_gate"].shape[-1]
    hid = _gateup(texp, first, nused, xs, p["w_expert_gate"][l].reshape(ne, D, F),
                  p["w_expert_up"][l].reshape(ne, D, F), tm=tm, tf=min(ts["expert_tf"], F))
    ys = _down(texp, first, nused, hid, p["w_expert_down"][l].reshape(ne, F, D), tm=tm,
               tn=ts["expert_tn"])
    return _combine(pos[0], pos[1], ys, x1, wts[0], wts[1], tc=ts["combine_tc"])


def kernel(x, mix_norm, w_in, diff_q_norm, diff_k_norm, diff_lambda, diff_subln, na_q_norm, na_k_norm, na_rpb, w_diff_out, w_na_out, w_out, ffn_norm, w_router_group, b_router_group, w_router_expert, b_router_expert, w_expert_gate, w_expert_up, w_expert_down):
    B, S, D = x.shape
    assert S % GRID_W == 0 and S % 128 == 0 and QKV_W % D == 0
    p = dict(mix_norm=mix_norm, w_in=w_in, diff_q_norm=diff_q_norm, diff_k_norm=diff_k_norm,
             diff_lambda=diff_lambda, diff_subln=diff_subln, na_q_norm=na_q_norm,
             na_k_norm=na_k_norm, na_rpb=na_rpb, w_diff_out=w_diff_out, w_na_out=w_na_out,
             w_out=w_out, ffn_norm=ffn_norm, w_router_group=w_router_group,
             b_router_group=b_router_group, w_router_expert=w_router_expert,
             b_router_expert=b_router_expert, w_expert_gate=w_expert_gate,
             w_expert_up=w_expert_up, w_expert_down=w_expert_down)
    x2 = x.reshape(B * S, D).astype(F32)
    for l in range(mix_norm.shape[0]):
        x2 = _layer(x2, p, l, batch=B, seq=S)
    return x2.reshape(B, S, D).astype(x.dtype)
```

```python
import functools
import math

import jax
import jax.numpy as jnp
import numpy as np
from jax import lax
from jax.experimental import pallas as pl
from jax.experimental.pallas import tpu as pltpu

F32 = jnp.float32
BF16 = jnp.bfloat16
I32 = jnp.int32
U32 = jnp.uint32

EPS = 1e-6
LOG2E = math.log2(math.e)
GRID_W = 64
DIFF_HEADS = 8
DIFF_QK_DIM = 64
NA_HEADS = 8
NA_DIM = 128
NA_WIN_ROWS = 8
NA_WIN_COLS = 16
TOP_K = 2
HEAD_W = 128
DQ_BLK, DK_BLK, DV_BLK = 0, DIFF_HEADS, 2 * DIFF_HEADS
NQ_BLK = 3 * DIFF_HEADS
NK_BLK = NQ_BLK + NA_HEADS
NV_BLK = NQ_BLK + 2 * NA_HEADS
QKV_W = (3 * DIFF_HEADS + 3 * NA_HEADS) * HEAD_W

ROUTER_ROWS = 40
VMEM_LIMIT = 56 * 1024 * 1024

_NT = (((1,), (1,)), ((), ()))


def _cparams(sem, vmem=VMEM_LIMIT):
    return pltpu.CompilerParams(dimension_semantics=sem, vmem_limit_bytes=vmem)


def _pack_halves(x):
    n = x.shape[1] // 2
    lo = lax.bitcast_convert_type(x[:, :n].astype(BF16).astype(F32), U32)
    hi = lax.bitcast_convert_type(x[:, n:].astype(BF16).astype(F32), U32)
    return (lo >> 16) | hi


def _unpack_halves(w):
    lo = lax.bitcast_convert_type(w << 16, F32)
    hi = lax.bitcast_convert_type(w & jnp.uint32(0xFFFF0000), F32)
    return lo, hi


def _inproj_body(x_ref, g_ref, w_ref, o_ref, h_scr, *, rows):
    @pl.when(pl.program_id(1) == 0)
    def _():
        def chunk(c, carry):
            r = pl.ds(pl.multiple_of(c * rows, rows), rows)
            x = x_ref[r, :]
            ms = jnp.mean(x * x, axis=-1, keepdims=True)
            h_scr[r, :] = (x * lax.rsqrt(ms + EPS) * g_ref[...]).astype(BF16)
            return carry
        lax.fori_loop(0, x_ref.shape[0] // rows, chunk, 0)

    o_ref[...] = jnp.dot(h_scr[...], w_ref[...], preferred_element_type=F32).astype(BF16)


def _inproj(x2, gain, w_bf16, *, tm, tn):
    T, D = x2.shape
    N = w_bf16.shape[1]
    return pl.pallas_call(
        functools.partial(_inproj_body, rows=128),
        grid=(T // tm, N // tn),
        in_specs=[pl.BlockSpec((tm, D), lambda i, j: (i, 0)),
                  pl.BlockSpec((1, D), lambda i, j: (0, 0)),
                  pl.BlockSpec((D, tn), lambda i, j: (0, j))],
        out_specs=pl.BlockSpec((tm, tn), lambda i, j: (i, j)),
        out_shape=jax.ShapeDtypeStruct((T, N), BF16),
        scratch_shapes=[pltpu.VMEM((tm, D), BF16)],
        compiler_params=_cparams(("parallel", "arbitrary")),
        name="inproj",
    )(x2, gain.reshape(1, D), w_bf16)


def _half_rms(x, gain, first):
    x2 = x * x
    s0 = jnp.sum(jnp.where(first, x2, 0.0), axis=-1, keepdims=True)
    s1 = jnp.sum(jnp.where(first, 0.0, x2), axis=-1, keepdims=True)
    inv = jnp.where(first, lax.rsqrt(s0 * (1.0 / DIFF_QK_DIM) + EPS),
                    lax.rsqrt(s1 * (1.0 / DIFF_QK_DIM) + EPS))
    return x * inv * gain


def _alibi_tiles(slopes, seq):
    nb = seq // HEAD_W
    off = np.arange(HEAD_W)
    delta = (np.arange(-(nb - 1), nb)[:, None, None] * HEAD_W + off[None, None, :] - off[None, :, None])
    dist = jnp.asarray(np.abs(delta).astype(np.float32))
    return (-LOG2E * slopes.astype(F32))[:, None, None, None] * dist


def _diff_body(lam_ref, q_ref, k_ref, v_ref, qg_ref, kg_ref, sg_ref, bias_ref, o_ref,
               qm_scr, kn_scr, vt_scr, s0_scr, s1_scr, p_scr, ot_scr, *, tq, seq, out_scale):
    nkb = seq // HEAD_W
    nq = seq // tq
    qblks = tq // HEAD_W
    groups = HEAD_W // 8
    lane = lax.broadcasted_iota(I32, (1, HEAD_W), 1)
    first = lane < DIFF_QK_DIM
    kn_scr[...] = _half_rms(k_ref[...].astype(F32), kg_ref[...], first).astype(BF16)
    vt_scr[:HEAD_W, :] = v_ref[...].astype(F32).T.astype(BF16)
    vt_scr[HEAD_W:, :] = jnp.ones((8, seq), BF16)
    q = _half_rms(q_ref[...].astype(F32), qg_ref[...], first) * (DIFF_QK_DIM ** -0.5 * LOG2E)
    qm_scr[0] = jnp.where(first, q, 0.0).astype(BF16)
    qm_scr[1] = jnp.where(first, 0.0, q).astype(BF16)
    lam = lam_ref[0]

    def stage(score, expo):
        mx = [jnp.full((8, tq), -jnp.inf, F32), jnp.full((8, tq), -jnp.inf, F32)]
        for c in range(nkb):
            ks = slice(c * HEAD_W, (c + 1) * HEAD_W)
            if score is not None:
                qa, dst = score
                rows = pl.ds(pl.multiple_of(qa * tq, tq), tq)
                kc = kn_scr[ks, :]
                tile0 = qa * qblks - c + (nkb - 1)
                bias = jnp.concatenate([bias_ref[0, tile0 + t] for t in range(qblks)], axis=1)
                for m in range(2):
                    s = lax.dot_general(kc, qm_scr[m, rows, :], _NT,
                                        preferred_element_type=F32) + bias
                    dst[m, ks, :] = s
                    mx[m] = jnp.maximum(mx[m], jnp.max(s.reshape(groups, 8, tq), axis=0))
            if expo is not None:
                src, mrow = expo
                for m in range(2):
                    p_scr[m, ks, :] = jnp.exp2((src[m, ks, :] - mrow[m]).astype(BF16))
        return tuple(jnp.max(a, axis=0, keepdims=True) for a in mx)

    def values(qb):
        outs = []
        for m in range(2):
            ol = jnp.dot(vt_scr[...], p_scr[m], preferred_element_type=F32)
            outs.append(ol[:HEAD_W] / ol[HEAD_W:HEAD_W + 1])
        ot_scr[qb] = outs[0] - lam * outs[1]

    mrow = stage((0, s0_scr), None)

    def pair(j, mrow):
        mrow1 = stage((2 * j + 1, s1_scr), (s0_scr, mrow))
        values(2 * j)
        mrow0 = stage((2 * j + 2, s0_scr), (s1_scr, mrow1))
        values(2 * j + 1)
        return mrow0
    mrow = lax.fori_loop(0, nq // 2 - 1, pair, mrow)
    mrow1 = stage((nq - 1, s1_scr), (s0_scr, mrow))
    values(nq - 2)
    stage(None, (s1_scr, mrow1))
    values(nq - 1)

    gain = sg_ref[...] * out_scale
    for i in range(nq):
        o = ot_scr[i]
        o = o * lax.rsqrt(jnp.mean(o * o, axis=0, keepdims=True) + EPS)
        o_ref[i * tq:(i + 1) * tq, :] = (o.T * gain).astype(BF16)


def _diff_attention(proj, bias_tiles, lam, q_gain, k_gain, sub_gain, *, batch, seq, tq, out_scale):
    T = proj.shape[0]
    smem = pl.BlockSpec(memory_space=pltpu.SMEM)
    vec = pl.BlockSpec((1, HEAD_W), lambda h, b: (0, 0))
    blk = lambda col: pl.BlockSpec((seq, HEAD_W), lambda h, b: (b, col + h))
    return pl.pallas_call(
        functools.partial(_diff_body, tq=tq, seq=seq, out_scale=out_scale),
        grid=(DIFF_HEADS, batch),
        in_specs=[smem, blk(DQ_BLK), blk(DK_BLK), blk(DV_BLK), vec, vec, vec,
                  pl.BlockSpec((1,) + bias_tiles.shape[1:], lambda h, b: (h, 0, 0, 0))],
        out_specs=pl.BlockSpec((seq, HEAD_W), lambda h, b: (b, h)),
        out_shape=jax.ShapeDtypeStruct((T, DIFF_HEADS * HEAD_W), BF16),
        scratch_shapes=[pltpu.VMEM((2, seq, HEAD_W), BF16), pltpu.VMEM((seq, HEAD_W), BF16),
                        pltpu.VMEM((HEAD_W + 8, seq), BF16),
                        pltpu.VMEM((2, seq, tq), F32), pltpu.VMEM((2, seq, tq), F32),
                        pltpu.VMEM((2, seq, tq), BF16), pltpu.VMEM((seq // tq, HEAD_W, tq), F32)],
        compiler_params=_cparams(("parallel", "parallel")),
        name="diff_attn",
    )(lam, proj, proj, proj, q_gain, k_gain, sub_gain, bias_tiles)


def _na_bias_table(rpb, rows):
    wr = min(NA_WIN_ROWS, rows)
    c = np.arange(GRID_W)
    col_start = np.clip(c - NA_WIN_COLS // 2, 0, GRID_W - NA_WIN_COLS)
    col_in = (c[None, :] >= col_start[:, None]) & (c[None, :] < col_start[:, None] + NA_WIN_COLS)
    dc = np.clip(c[None, :] - c[:, None], -(NA_WIN_COLS - 1), NA_WIN_COLS - 1) + (NA_WIN_COLS - 1)
    by_row = jnp.stack([rpb[:, NA_WIN_ROWS - 1 - pt:NA_WIN_ROWS - 1 - pt + wr, :] for pt in range(wr)],
                       axis=1).astype(F32)
    sel = (dc[:, :, None] == np.arange(2 * NA_WIN_COLS - 1)[None, None, :]).astype(np.float32)
    b = jnp.einsum("hpwd,ckd->hpcwk", by_row, jnp.asarray(sel), precision=lax.Precision.HIGHEST)
    b = jnp.where(jnp.asarray(col_in)[None, None, :, None, :], b, -jnp.inf)
    return b.reshape(rpb.shape[0], wr, GRID_W, wr * GRID_W)


def _na_body(q_ref, k_ref, v_ref, qg_ref, kg_ref, bias_ref, o_ref, qn_scr, kn_scr, *, rows, wr):
    def rms(x, g):
        return x * lax.rsqrt(jnp.mean(x * x, axis=-1, keepdims=True) + EPS) * g

    qn_scr[...] = (rms(q_ref[...].astype(F32), qg_ref[...]) * (NA_DIM ** -0.5)).astype(BF16)
    kn_scr[...] = rms(k_ref[...].astype(F32), kg_ref[...]).astype(BF16)
    win = wr * GRID_W

    def one_row(r, start_row, pattern):
        qs = slice(r * GRID_W, (r + 1) * GRID_W)
        ks = slice(start_row * GRID_W, start_row * GRID_W + win)
        s = lax.dot_general(qn_scr[qs, :], kn_scr[ks, :], _NT, preferred_element_type=F32)
        s = s + bias_ref[0, pattern]
        m = jnp.max(s, axis=-1, keepdims=True)
        p = jnp.exp(s - m)
        l = jnp.sum(p, axis=-1, keepdims=True)
        o = jnp.dot(p.astype(BF16), v_ref[ks, :], preferred_element_type=F32)
        o_ref[qs, :] = (o / l).astype(BF16)

    for r in range(rows):
        start_row = min(max(r - wr // 2, 0), rows - wr)
        one_row(r, start_row, r - start_row)


def _na_attention(proj, q_gain, k_gain, bias_tab, *, batch, seq):
    T = proj.shape[0]
    rows = seq // GRID_W
    wr = min(NA_WIN_ROWS, rows)
    vec = pl.BlockSpec((1, NA_DIM), lambda h, b: (0, 0))
    return pl.pallas_call(
        functools.partial(_na_body, rows=rows, wr=wr),
        grid=(NA_HEADS, batch),
        in_specs=[pl.BlockSpec((seq, HEAD_W), lambda h, b: (b, NQ_BLK + h)),
                  pl.BlockSpec((seq, HEAD_W), lambda h, b: (b, NK_BLK + h)),
                  pl.BlockSpec((seq, HEAD_W), lambda h, b: (b, NV_BLK + h)),
                  vec, vec,
                  pl.BlockSpec((1, wr, GRID_W, wr * GRID_W), lambda h, b: (h, 0, 0, 0))],
        out_specs=pl.BlockSpec((seq, HEAD_W), lambda h, b: (b, h)),
        out_shape=jax.ShapeDtypeStruct((T, NA_HEADS * NA_DIM), BF16),
        scratch_shapes=[pltpu.VMEM((seq, HEAD_W), BF16), pltpu.VMEM((seq, HEAD_W), BF16)],
        compiler_params=_cparams(("parallel", "parallel")),
        name="na_attn",
    )(proj, proj, proj, q_gain, k_gain, bias_tab)


def _outproj_body(oa_ref, ob_ref, ga_ref, gb_ref, x_ref, wd_ref, wn_ref, wo_ref, fg_ref, wr_ref,
                  x1_ref, h2_ref, lt_ref):
    ya = jnp.dot(oa_ref[...], wd_ref[...], preferred_element_type=F32)
    yb = jnp.dot(ob_ref[...], wn_ref[...], preferred_element_type=F32)
    merged = (jax.nn.sigmoid(ga_ref[...].astype(F32)) * ya
              + jax.nn.sigmoid(gb_ref[...].astype(F32)) * yb)
    x1 = x_ref[...] + jnp.dot(merged.astype(BF16), wo_ref[...], preferred_element_type=F32)
    x1_ref[...] = x1
    h2 = x1 * lax.rsqrt(jnp.mean(x1 * x1, axis=-1, keepdims=True) + EPS) * fg_ref[...]
    h2_ref[...] = _pack_halves(h2)
    lt_ref[...] = lax.dot_general(wr_ref[...], h2, _NT, precision=lax.Precision.HIGHEST,
                                  preferred_element_type=F32)


def _outproj(oa, ob, proj, x2, wd, wn, wo, ffn_gain, wr_t, *, tm):
    T, D = x2.shape
    gate_blk = QKV_W // D
    const = lambda shape: pl.BlockSpec(shape, lambda i: (0, 0), pipeline_mode=pl.Buffered(1))
    return pl.pallas_call(
        _outproj_body,
        grid=(T // tm,),
        in_specs=[pl.BlockSpec((tm, oa.shape[1]), lambda i: (i, 0)),
                  pl.BlockSpec((tm, ob.shape[1]), lambda i: (i, 0)),
                  pl.BlockSpec((tm, D), lambda i: (i, gate_blk)),
                  pl.BlockSpec((tm, D), lambda i: (i, gate_blk + 1)),
                  pl.BlockSpec((tm, D), lambda i: (i, 0)),
                  const(wd.shape), const(wn.shape), const(wo.shape),
                  const((1, D)), const(wr_t.shape)],
        out_specs=[pl.BlockSpec((tm, D), lambda i: (i, 0)),
                   pl.BlockSpec((tm, D // 2), lambda i: (i, 0)),
                   pl.BlockSpec((ROUTER_ROWS, tm), lambda i: (0, i))],
        out_shape=[jax.ShapeDtypeStruct((T, D), F32),
                   jax.ShapeDtypeStruct((T, D // 2), U32),
                   jax.ShapeDtypeStruct((ROUTER_ROWS, T), F32)],
        compiler_params=_cparams(("parallel",)),
        name="outproj",
    )(oa, ob, proj, proj, x2, wd, wn, wo, ffn_gain.reshape(1, D), wr_t)


def _route_body(lt_ref, bias_ref, pos_ref, wts_ref, meta_ref, cnt_scr, carry_scr,
                *, n_groups, n_exp, tm, tc):
    ne = n_groups * n_exp
    phase = pl.program_id(0)
    step = pl.program_id(1)

    l = lt_ref[...] + bias_ref[...]
    el = l[0:ne]
    gl = l[ne:ne + n_groups]
    gmax = jnp.max(gl, axis=0, keepdims=True)
    gi = lax.broadcasted_iota(I32, gl.shape, 0)
    gsel = jnp.min(jnp.where(gl == gmax, gi, n_groups), axis=0, keepdims=True)
    gw = 1.0 / jnp.sum(jnp.exp(gl - gmax), axis=0, keepdims=True)
    ei = lax.broadcasted_iota(I32, el.shape, 0)
    lo = gsel * n_exp
    elm = jnp.where((ei >= lo) & (ei < lo + n_exp), el, -jnp.inf)
    m1 = jnp.max(elm, axis=0, keepdims=True)
    i1 = jnp.min(jnp.where(elm == m1, ei, ne), axis=0, keepdims=True)
    elm2 = jnp.where(ei == i1, -jnp.inf, elm)
    m2 = jnp.max(elm2, axis=0, keepdims=True)
    i2 = jnp.min(jnp.where(elm2 == m2, ei, ne), axis=0, keepdims=True)
    r = jnp.exp(m2 - m1)
    w1 = gw / (1.0 + r)
    w2 = w1 * r
    oh1 = ei == i1
    oh2 = ei == i2
    both = jnp.where(oh1 | oh2, 1.0, 0.0)

    @pl.when((phase == 0) & (step == 0))
    def _():
        cnt_scr[...] = jnp.zeros_like(cnt_scr)
        carry_scr[...] = jnp.zeros_like(carry_scr)

    @pl.when(phase == 0)
    def _():
        cnt_scr[...] += jnp.sum(both, axis=1, keepdims=True)

    @pl.when(phase == 1)
    def _():
        cnt = cnt_scr[...]
        tiles = jnp.floor((cnt + (tm - 1)) * (1.0 / tm))
        ea = lax.broadcasted_iota(I32, (ne, ne), 0)
        eb = lax.broadcasted_iota(I32, (ne, ne), 1)
        lower = jnp.where(eb < ea, 1.0, 0.0).astype(BF16)
        tiles_b = jnp.broadcast_to(tiles, (ne, 128)).astype(BF16)
        start_t = jnp.dot(lower, tiles_b, preferred_element_type=F32)[:, 0:1]
        ta = lax.broadcasted_iota(I32, (tc, tc), 0)
        tb = lax.broadcasted_iota(I32, (tc, tc), 1)
        upper = jnp.where(ta < tb, 1.0, 0.0).astype(BF16)
        prefix = jnp.dot(both.astype(BF16), upper, preferred_element_type=F32)
        base = prefix + carry_scr[...] + start_t * tm
        pos1 = jnp.sum(jnp.where(oh1, base, 0.0), axis=0, keepdims=True)
        pos2 = jnp.sum(jnp.where(oh2, base, 0.0), axis=0, keepdims=True)
        carry_scr[...] += jnp.sum(both, axis=1, keepdims=True)
        pos_ref[...] = jnp.zeros_like(pos_ref)
        pos_ref[0:1, :] = pos1.astype(I32)
        pos_ref[1:2, :] = pos2.astype(I32)
        wts_ref[...] = jnp.zeros_like(wts_ref)
        wts_ref[0:1, :] = w1
        wts_ref[1:2, :] = w2

        nt = meta_ref.shape[1]
        tj = lax.broadcasted_iota(I32, (ne, nt), 1).astype(F32)
        end_t = start_t + tiles
        texp = jnp.sum(jnp.where(end_t <= tj, 1.0, 0.0), axis=0, keepdims=True)
        is_first = jnp.sum(jnp.where((start_t == tj) & (tiles > 0.0), 1.0, 0.0),
                           axis=0, keepdims=True)
        is_last = jnp.sum(jnp.where((end_t - 1.0 == tj) & (tiles > 0.0), 1.0, 0.0),
                          axis=0, keepdims=True)
        nused = jnp.broadcast_to(jnp.sum(tiles, axis=0, keepdims=True), (1, nt))
        meta_ref[...] = jnp.zeros_like(meta_ref)
        meta_ref[0:1, :] = jnp.minimum(texp, ne - 1.0).astype(I32)
        meta_ref[1:2, :] = is_first.astype(I32)
        fill = is_last + jnp.where(tj[0:1, :] >= nused, 1.0, 0.0)
        meta_ref[2:3, :] = fill.astype(I32)
        meta_ref[3:4, :] = nused.astype(I32)


def _route(lt, bias_col, *, n_groups, n_exp, tm, n_tiles, tc):
    T = lt.shape[1]
    ne = n_groups * n_exp
    nt_pad = -(-n_tiles // 128) * 128
    return pl.pallas_call(
        functools.partial(_route_body, n_groups=n_groups, n_exp=n_exp, tm=tm, tc=tc),
        grid=(2, T // tc),
        in_specs=[pl.BlockSpec((ROUTER_ROWS, tc), lambda p, i: (0, i)),
                  pl.BlockSpec((ROUTER_ROWS, 1), lambda p, i: (0, 0))],
        out_specs=[pl.BlockSpec((8, tc), lambda p, i: (0, i * p)),
                   pl.BlockSpec((8, tc), lambda p, i: (0, i * p)),
                   pl.BlockSpec((8, nt_pad), lambda p, i: (0, 0))],
        out_shape=[jax.ShapeDtypeStruct((8, T), I32),
                   jax.ShapeDtypeStruct((8, T), F32),
                   jax.ShapeDtypeStruct((8, nt_pad), I32)],
        scratch_shapes=[pltpu.VMEM((ne, 1), F32), pltpu.VMEM((ne, 1), F32)],
        compiler_params=_cparams(("arbitrary", "arbitrary")),
        name="route",
    )(lt, bias_col)


DMA_UNROLL = 8


def _dispatch_body(fill_ref, pos1_ref, pos2_ref, h_ref, xs_ref, zero_scr, sem, zsem, *, td, tm):
    n_tiles = fill_ref.shape[0]

    def zero_copy(j):
        return pltpu.make_async_copy(zero_scr, xs_ref.at[pl.ds(pl.multiple_of(j * tm, tm), tm), :], zsem)

    @pl.when(pl.program_id(0) == 0)
    def _():
        zero_scr[...] = jnp.zeros_like(zero_scr)

        def start(j, carry):
            @pl.when(fill_ref[j] == 1)
            def _():
                zero_copy(j).start()
            return carry
        lax.fori_loop(0, n_tiles, start, 0)

        def wait(j, carry):
            @pl.when(fill_ref[j] == 1)
            def _():
                zero_copy(j).wait()
            return carry
        lax.fori_loop(0, n_tiles, wait, 0)

    def row_copy(t, pos):
        return pltpu.make_async_copy(h_ref.at[pl.ds(t, 1), :], xs_ref.at[pl.ds(pos, 1), :], sem)

    def issue(t, carry):
        row_copy(t, pos1_ref[0, 0, t]).start()
        row_copy(t, pos2_ref[0, 0, t]).start()
        return carry
    lax.fori_loop(0, td, issue, 0, unroll=DMA_UNROLL)

    def drain(t, carry):
        row_copy(t, 0).wait()
        row_copy(t, 0).wait()
        return carry
    lax.fori_loop(0, td, drain, 0, unroll=DMA_UNROLL)


def _dispatch(fill, pos1, pos2, h2p, *, n_rows, td, tm):
    T, W = h2p.shape
    nblk = T // td
    smem3 = pl.BlockSpec((1, 1, td), lambda i, fl: (i, 0, 0), memory_space=pltpu.SMEM)
    grid_spec = pltpu.PrefetchScalarGridSpec(
        num_scalar_prefetch=1,
        grid=(nblk,),
        in_specs=[smem3, smem3, pl.BlockSpec((td, W), lambda i, fl: (i, 0))],
        out_specs=pl.BlockSpec(memory_space=pl.ANY),
        scratch_shapes=[pltpu.VMEM((tm, W), U32), pltpu.SemaphoreType.DMA(()),
                        pltpu.SemaphoreType.DMA(())],
    )
    return pl.pallas_call(
        functools.partial(_dispatch_body, td=td, tm=tm),
        grid_spec=grid_spec,
        out_shape=jax.ShapeDtypeStruct((n_rows, W), U32),
        compiler_params=_cparams(("arbitrary",)),
        name="dispatch",
    )(fill, pos1.reshape(nblk, 1, td), pos2.reshape(nblk, 1, td), h2p)


def _gateup_body(texp_ref, first_ref, nused_ref, x_ref, wg_ref, wu_ref, hid_ref, wg_scr, wu_scr):
    j = pl.program_id(0)

    @pl.when(j < nused_ref[0])
    def _():
        @pl.when(first_ref[j] == 1)
        def _():
            wg_scr[...] = wg_ref[0].astype(BF16)
            wu_scr[...] = wu_ref[0].astype(BF16)
        lo, hi = _unpack_halves(x_ref[...])
        x = jnp.concatenate([lo, hi], axis=1).astype(BF16)
        g = jnp.dot(x, wg_scr[...], preferred_element_type=F32)
        u = jnp.dot(x, wu_scr[...], preferred_element_type=F32)
        hid_ref[...] = (g * jax.nn.sigmoid(g) * u).astype(BF16)

    @pl.when(j >= nused_ref[0])
    def _():
        hid_ref[...] = jnp.zeros_like(hid_ref)


def _used_tile(j, nused):
    return jnp.minimum(j, nused[0] - 1)


def _gateup(texp, first, nused, xs, wg, wu, *, tm):
    P, W = xs.shape
    D, F = wg.shape[1:]
    grid_spec = pltpu.PrefetchScalarGridSpec(
        num_scalar_prefetch=3,
        grid=(P // tm,),
        in_specs=[pl.BlockSpec((tm, W), lambda j, te, fi, nu: (_used_tile(j, nu), 0)),
                  pl.BlockSpec((1, D, F), lambda j, te, fi, nu: (te[_used_tile(j, nu)], 0, 0)),
                  pl.BlockSpec((1, D, F), lambda j, te, fi, nu: (te[_used_tile(j, nu)], 0, 0))],
        out_specs=pl.BlockSpec((tm, F), lambda j, te, fi, nu: (j, 0)),
        scratch_shapes=[pltpu.VMEM((D, F), BF16), pltpu.VMEM((D, F), BF16)],
    )
    return pl.pallas_call(
        _gateup_body,
        grid_spec=grid_spec,
        out_shape=jax.ShapeDtypeStruct((P, F), BF16),
        compiler_params=_cparams(("arbitrary",)),
        name="expert_gateup",
    )(texp, first, nused, xs, wg, wu)


def _down_body(texp_ref, first_ref, nused_ref, hid_ref, wd_ref, y_ref, wd_scr):
    j = pl.program_id(0)

    @pl.when(j < nused_ref[0])
    def _():
        @pl.when(first_ref[j] == 1)
        def _():
            wd_scr[...] = wd_ref[0].astype(BF16)
        y_ref[...] = _pack_halves(jnp.dot(hid_ref[...], wd_scr[...], preferred_element_type=F32))

    @pl.when(j >= nused_ref[0])
    def _():
        y_ref[...] = jnp.zeros_like(y_ref)


def _down(texp, first, nused, hid, wd, *, tm):
    P, F = hid.shape
    D = wd.shape[2]
    grid_spec = pltpu.PrefetchScalarGridSpec(
        num_scalar_prefetch=3,
        grid=(P // tm,),
        in_specs=[pl.BlockSpec((tm, F), lambda j, te, fi, nu: (_used_tile(j, nu), 0)),
                  pl.BlockSpec((1, F, D), lambda j, te, fi, nu: (te[_used_tile(j, nu)], 0, 0))],
        out_specs=pl.BlockSpec((tm, D // 2), lambda j, te, fi, nu: (j, 0)),
        scratch_shapes=[pltpu.VMEM((F, D), BF16)],
    )
    return pl.pallas_call(
        _down_body,
        grid_spec=grid_spec,
        out_shape=jax.ShapeDtypeStruct((P, D // 2), U32),
        compiler_params=_cparams(("arbitrary",)),
        name="expert_down",
    )(texp, first, nused, hid, wd)


def _combine_body(pos1_ref, pos2_ref, ys_ref, x1_ref, w1_ref, w2_ref, o_ref, r1_scr, r2_scr, sem, *, tc):
    def row_copy(pos, dst, t):
        return pltpu.make_async_copy(ys_ref.at[pl.ds(pos, 1), :], dst.at[pl.ds(t, 1), :], sem)

    def issue(t, carry):
        row_copy(pos1_ref[0, 0, t], r1_scr, t).start()
        row_copy(pos2_ref[0, 0, t], r2_scr, t).start()
        return carry
    lax.fori_loop(0, tc, issue, 0, unroll=DMA_UNROLL)

    def drain(t, carry):
        row_copy(0, r1_scr, t).wait()
        row_copy(0, r2_scr, t).wait()
        return carry
    lax.fori_loop(0, tc, drain, 0, unroll=DMA_UNROLL)

    half = o_ref.shape[1] // 2
    lo1, hi1 = _unpack_halves(r1_scr[...])
    lo2, hi2 = _unpack_halves(r2_scr[...])
    w1 = w1_ref[...]
    w2 = w2_ref[...]
    o_ref[:, :half] = x1_ref[:, :half] + w1 * lo1 + w2 * lo2
    o_ref[:, half:] = x1_ref[:, half:] + w1 * hi1 + w2 * hi2


def _combine(pos1, pos2, ys, x1, w1, w2, *, tc):
    T, D = x1.shape
    nblk = T // tc
    smem3 = pl.BlockSpec((1, 1, tc), lambda i: (i, 0, 0), memory_space=pltpu.SMEM)
    col = pl.BlockSpec((tc, 1), lambda i: (i, 0))
    return pl.pallas_call(
        functools.partial(_combine_body, tc=tc),
        grid=(nblk,),
        in_specs=[smem3, smem3, pl.BlockSpec(memory_space=pl.ANY),
                  pl.BlockSpec((tc, D), lambda i: (i, 0)), col, col],
        out_specs=pl.BlockSpec((tc, D), lambda i: (i, 0)),
        out_shape=jax.ShapeDtypeStruct((T, D), F32),
        scratch_shapes=[pltpu.VMEM((tc, D // 2), U32), pltpu.VMEM((tc, D // 2), U32),
                        pltpu.SemaphoreType.DMA(())],
        compiler_params=_cparams(("arbitrary",)),
        name="combine",
    )(pos1.reshape(nblk, 1, tc), pos2.reshape(nblk, 1, tc), ys, x1, w1.reshape(T, 1), w2.reshape(T, 1))


def _tile_sizes(T, D, seq, in_w):
    pick = lambda n, pref: pref if n % pref == 0 else n
    return dict(
        inproj_tm=pick(T, 1024), inproj_tn=next(t for t in (1024, 512, 256, 128) if in_w % t == 0),
        diff_tq=pick(seq, 256),
        outproj_tm=pick(T, 256),
        route_tc=pick(T, 512),
        expert_tm=256,
        dispatch_td=pick(T, 256), combine_tc=pick(T, 256),
    )


def _layer(x2, p, l, *, batch, seq):
    T, D = x2.shape
    ts = _tile_sizes(T, D, seq, p["w_in"].shape[-1])
    lambda_init = 0.8 - 0.6 * math.exp(-0.3 * l)
    n_groups, n_exp = p["b_router_expert"].shape[1:]
    ne = n_groups * n_exp
    rows = seq // GRID_W

    proj = _inproj(x2, p["mix_norm"][l], p["w_in"][l].astype(BF16),
                   tm=ts["inproj_tm"], tn=ts["inproj_tn"])

    lam = p["diff_lambda"][l].astype(F32)
    lam_full = (jnp.exp(jnp.sum(lam[0] * lam[1])) - jnp.exp(jnp.sum(lam[2] * lam[3]))
                + lambda_init).reshape(1)
    slopes = 2.0 ** (-8.0 * jnp.arange(1, DIFF_HEADS + 1, dtype=F32) / DIFF_HEADS)
    two = lambda g: jnp.tile(g.astype(F32), 2).reshape(1, HEAD_W)
    oa = _diff_attention(proj, _alibi_tiles(slopes, seq), lam_full, two(p["diff_q_norm"][l]), two(p["diff_k_norm"][l]),
                         p["diff_subln"][l].astype(F32).reshape(1, HEAD_W),
                         batch=batch, seq=seq, tq=ts["diff_tq"], out_scale=1.0 - lambda_init)

    ob = _na_attention(proj, p["na_q_norm"][l].astype(F32).reshape(1, NA_DIM),
                       p["na_k_norm"][l].astype(F32).reshape(1, NA_DIM),
                       _na_bias_table(p["na_rpb"][l], rows), batch=batch, seq=seq)

    wr_t = jnp.concatenate([p["w_router_expert"][l].T, p["w_router_group"][l].T,
                            jnp.zeros((ROUTER_ROWS - ne - n_groups, D), F32)], axis=0).astype(F32)
    b_col = jnp.concatenate([p["b_router_expert"][l].reshape(ne), p["b_router_group"][l],
                             jnp.zeros((ROUTER_ROWS - ne - n_groups,), F32)]).astype(F32).reshape(ROUTER_ROWS, 1)
    x1, h2, lt = _outproj(oa, ob, proj, x2, p["w_diff_out"][l].astype(BF16),
                          p["w_na_out"][l].astype(BF16), p["w_out"][l].astype(BF16),
                          p["ffn_norm"][l].astype(F32), wr_t, tm=ts["outproj_tm"])

    tm = ts["expert_tm"]
    n_tiles = (T * TOP_K) // tm + ne
    pos, wts, meta = _route(lt, b_col, n_groups=n_groups, n_exp=n_exp, tm=tm, n_tiles=n_tiles,
                            tc=ts["route_tc"])
    texp, first, fill, nused = meta[0, :n_tiles], meta[1, :n_tiles], meta[2, :n_tiles], meta[3, :1]

    xs = _dispatch(fill, pos[0], pos[1], h2, n_rows=n_tiles * tm, td=ts["dispatch_td"], tm=tm)
    F = p["w_expert_gate"].shape[-1]
    hid = _gateup(texp, first, nused, xs, p["w_expert_gate"][l].reshape(ne, D, F),
                  p["w_expert_up"][l].reshape(ne, D, F), tm=tm)
    ys = _down(texp, first, nused, hid, p["w_expert_down"][l].reshape(ne, F, D), tm=tm)
    return _combine(pos[0], pos[1], ys, x1, wts[0], wts[1], tc=ts["combine_tc"])


def kernel(x, mix_norm, w_in, diff_q_norm, diff_k_norm, diff_lambda, diff_subln, na_q_norm, na_k_norm, na_rpb, w_diff_out, w_na_out, w_out, ffn_norm, w_router_group, b_router_group, w_router_expert, b_router_expert, w_expert_gate, w_expert_up, w_expert_down):
    B, S, D = x.shape
    assert S % GRID_W == 0 and S % 128 == 0 and QKV_W % D == 0
    p = dict(mix_norm=mix_norm, w_in=w_in, diff_q_norm=diff_q_norm, diff_k_norm=diff_k_norm,
             diff_lambda=diff_lambda, diff_subln=diff_subln, na_q_norm=na_q_norm,
             na_k_norm=na_k_norm, na_rpb=na_rpb, w_diff_out=w_diff_out, w_na_out=w_na_out,
             w_out=w_out, ffn_norm=ffn_norm, w_router_group=w_router_group,
             b_router_group=b_router_group, w_router_expert=w_router_expert,
             b_router_expert=b_router_expert, w_expert_gate=w_expert_gate,
             w_expert_up=w_expert_up, w_expert_down=w_expert_down)
    x2 = x.reshape(B * S, D).astype(F32)
    for l in range(mix_norm.shape[0]):
        x2 = _layer(x2, p, l, batch=B, seq=S)
    return x2.reshape(B, S, D).astype(x.dtype)
```

```python
import functools
import math

import jax
import jax.numpy as jnp
import numpy as np
from jax import lax
from jax.experimental import pallas as pl
from jax.experimental.pallas import tpu as pltpu

F32 = jnp.float32
BF16 = jnp.bfloat16
I32 = jnp.int32

EPS = 1e-6
LOG2E = math.log2(math.e)
GRID_W = 64
DIFF_HEADS = 8
DIFF_QK_DIM = 64
NA_HEADS = 8
NA_DIM = 128
NA_WIN_ROWS = 8
NA_WIN_COLS = 16
TOP_K = 2
HEAD_W = 128
DQ_BLK, DK_BLK, DV_BLK = 0, DIFF_HEADS, 2 * DIFF_HEADS
NQ_BLK = 3 * DIFF_HEADS
NK_BLK = NQ_BLK + NA_HEADS
NV_BLK = NQ_BLK + 2 * NA_HEADS
QKV_W = (3 * DIFF_HEADS + 3 * NA_HEADS) * HEAD_W

ROUTER_ROWS = 48
VMEM_LIMIT = 56 * 1024 * 1024

_NT = (((1,), (1,)), ((), ()))


def _cparams(sem, vmem=VMEM_LIMIT):
    return pltpu.CompilerParams(dimension_semantics=sem, vmem_limit_bytes=vmem)


def _inproj_body(x_ref, g_ref, w_ref, o_ref, h_scr, *, rows):
    @pl.when(pl.program_id(1) == 0)
    def _():
        def chunk(c, carry):
            r = pl.ds(pl.multiple_of(c * rows, rows), rows)
            x = x_ref[r, :]
            ms = jnp.mean(x * x, axis=-1, keepdims=True)
            h_scr[r, :] = (x * lax.rsqrt(ms + EPS) * g_ref[...]).astype(BF16)
            return carry
        lax.fori_loop(0, x_ref.shape[0] // rows, chunk, 0)

    o_ref[...] = jnp.dot(h_scr[...], w_ref[...], preferred_element_type=F32).astype(BF16)


def _inproj(x2, gain, w_bf16, *, tm, tn):
    T, D = x2.shape
    N = w_bf16.shape[1]
    return pl.pallas_call(
        functools.partial(_inproj_body, rows=128),
        grid=(T // tm, N // tn),
        in_specs=[pl.BlockSpec((tm, D), lambda i, j: (i, 0)),
                  pl.BlockSpec((1, D), lambda i, j: (0, 0)),
                  pl.BlockSpec((D, tn), lambda i, j: (0, j))],
        out_specs=pl.BlockSpec((tm, tn), lambda i, j: (i, j)),
        out_shape=jax.ShapeDtypeStruct((T, N), BF16),
        scratch_shapes=[pltpu.VMEM((tm, D), BF16)],
        compiler_params=_cparams(("parallel", "arbitrary")),
        name="inproj",
    )(x2, gain.reshape(1, D), w_bf16)


def _half_rms(x, gain, first):
    x2 = x * x
    s0 = jnp.sum(jnp.where(first, x2, 0.0), axis=-1, keepdims=True)
    s1 = jnp.sum(jnp.where(first, 0.0, x2), axis=-1, keepdims=True)
    inv = jnp.where(first, lax.rsqrt(s0 * (1.0 / DIFF_QK_DIM) + EPS),
                    lax.rsqrt(s1 * (1.0 / DIFF_QK_DIM) + EPS))
    return x * inv * gain


def _alibi_tiles(slopes, seq):
    nb = seq // HEAD_W
    off = np.arange(HEAD_W)
    delta = (np.arange(-(nb - 1), nb)[:, None, None] * HEAD_W + off[None, None, :] - off[None, :, None])
    dist = jnp.asarray(np.abs(delta).astype(np.float32))
    return (-LOG2E * slopes.astype(F32))[:, None, None, None] * dist


def _diff_body(lam_ref, q_ref, k_ref, v_ref, qg_ref, kg_ref, sg_ref, bias_ref, o_ref,
               qm_scr, kn_scr, vt_scr, s0_scr, s1_scr, p_scr, ot_scr, *, tq, seq, out_scale):
    nkb = seq // HEAD_W
    nq = seq // tq
    qblks = tq // HEAD_W
    groups = HEAD_W // 8
    lane = lax.broadcasted_iota(I32, (1, HEAD_W), 1)
    first = lane < DIFF_QK_DIM
    kn_scr[...] = _half_rms(k_ref[...].astype(F32), kg_ref[...], first).astype(BF16)
    vt_scr[:HEAD_W, :] = v_ref[...].astype(F32).T.astype(BF16)
    vt_scr[HEAD_W:, :] = jnp.ones((8, seq), BF16)
    q = _half_rms(q_ref[...].astype(F32), qg_ref[...], first) * (DIFF_QK_DIM ** -0.5 * LOG2E)
    qm_scr[0] = jnp.where(first, q, 0.0).astype(BF16)
    qm_scr[1] = jnp.where(first, 0.0, q).astype(BF16)
    lam = lam_ref[0]

    def stage(score, expo):
        mx = [jnp.full((8, tq), -jnp.inf, F32), jnp.full((8, tq), -jnp.inf, F32)]
        for c in range(nkb):
            ks = slice(c * HEAD_W, (c + 1) * HEAD_W)
            if score is not None:
                qa, dst = score
                rows = pl.ds(pl.multiple_of(qa * tq, tq), tq)
                kc = kn_scr[ks, :]
                tile0 = qa * qblks - c + (nkb - 1)
                bias = jnp.concatenate([bias_ref[0, tile0 + t] for t in range(qblks)], axis=1)
                for m in range(2):
                    s = lax.dot_general(kc, qm_scr[m, rows, :], _NT,
                                        preferred_element_type=F32) + bias
                    dst[m, ks, :] = s
                    mx[m] = jnp.maximum(mx[m], jnp.max(s.reshape(groups, 8, tq), axis=0))
            if expo is not None:
                src, mrow = expo
                for m in range(2):
                    p_scr[m, ks, :] = jnp.exp2((src[m, ks, :] - mrow[m]).astype(BF16))
        return tuple(jnp.max(a, axis=0, keepdims=True) for a in mx)

    def values(qb):
        outs = []
        for m in range(2):
            ol = jnp.dot(vt_scr[...], p_scr[m], preferred_element_type=F32)
            outs.append(ol[:HEAD_W] / ol[HEAD_W:HEAD_W + 1])
        ot_scr[qb] = outs[0] - lam * outs[1]

    mrow = stage((0, s0_scr), None)

    def pair(j, mrow):
        mrow1 = stage((2 * j + 1, s1_scr), (s0_scr, mrow))
        values(2 * j)
        mrow0 = stage((2 * j + 2, s0_scr), (s1_scr, mrow1))
        values(2 * j + 1)
        return mrow0
    mrow = lax.fori_loop(0, nq // 2 - 1, pair, mrow)
    mrow1 = stage((nq - 1, s1_scr), (s0_scr, mrow))
    values(nq - 2)
    stage(None, (s1_scr, mrow1))
    values(nq - 1)

    gain = sg_ref[...] * out_scale
    for i in range(nq):
        o = ot_scr[i]
        o = o * lax.rsqrt(jnp.mean(o * o, axis=0, keepdims=True) + EPS)
        o_ref[i * tq:(i + 1) * tq, :] = (o.T * gain).astype(BF16)


def _diff_attention(proj, bias_tiles, lam, q_gain, k_gain, sub_gain, *, batch, seq, tq, out_scale):
    T = proj.shape[0]
    smem = pl.BlockSpec(memory_space=pltpu.SMEM)
    vec = pl.BlockSpec((1, HEAD_W), lambda h, b: (0, 0))
    blk = lambda col: pl.BlockSpec((seq, HEAD_W), lambda h, b: (b, col + h))
    return pl.pallas_call(
        functools.partial(_diff_body, tq=tq, seq=seq, out_scale=out_scale),
        grid=(DIFF_HEADS, batch),
        in_specs=[smem, blk(DQ_BLK), blk(DK_BLK), blk(DV_BLK), vec, vec, vec,
                  pl.BlockSpec((1,) + bias_tiles.shape[1:], lambda h, b: (h, 0, 0, 0))],
        out_specs=pl.BlockSpec((seq, HEAD_W), lambda h, b: (b, h)),
        out_shape=jax.ShapeDtypeStruct((T, DIFF_HEADS * HEAD_W), BF16),
        scratch_shapes=[pltpu.VMEM((2, seq, HEAD_W), BF16), pltpu.VMEM((seq, HEAD_W), BF16),
                        pltpu.VMEM((HEAD_W + 8, seq), BF16),
                        pltpu.VMEM((2, seq, tq), F32), pltpu.VMEM((2, seq, tq), F32),
                        pltpu.VMEM((2, seq, tq), BF16), pltpu.VMEM((seq // tq, HEAD_W, tq), F32)],
        compiler_params=_cparams(("parallel", "parallel")),
        name="diff_attn",
    )(lam, proj, proj, proj, q_gain, k_gain, sub_gain, bias_tiles)


NA_GROUP_ROWS = 4


def _na_plan(rows):
    wr = min(NA_WIN_ROWS, rows)
    gs = NA_GROUP_ROWS if rows % NA_GROUP_ROWS == 0 else 1
    ww = wr + gs - 1
    ww = min(ww + ww % 2, rows)
    row_start = lambda r: min(max(r - wr // 2, 0), rows - wr)
    groups, patterns = [], {}
    for g in range(rows // gs):
        ws = min(row_start(g * gs), rows - ww)
        key = tuple((row_start(r) - ws, r - ws) for r in range(g * gs, (g + 1) * gs))
        groups.append((ws, patterns.setdefault(key, len(patterns))))
    return wr, gs, ww, groups, list(patterns)


def _na_bias_table(rpb, rows):
    wr, gs, ww, _, patterns = _na_plan(rows)
    H = rpb.shape[0]
    c = np.arange(GRID_W)
    col_start = np.clip(c - NA_WIN_COLS // 2, 0, GRID_W - NA_WIN_COLS)
    col_in = (c[None, :] >= col_start[:, None]) & (c[None, :] < col_start[:, None] + NA_WIN_COLS)
    dc = np.clip(c[None, :] - c[:, None], -(NA_WIN_COLS - 1), NA_WIN_COLS - 1) + (NA_WIN_COLS - 1)
    sel = jnp.asarray((dc[:, :, None] == np.arange(2 * NA_WIN_COLS - 1)[None, None, :]).astype(np.float32))
    scaled = rpb.astype(F32) * LOG2E
    pats = []
    for key in patterns:
        blocks = []
        for win_off, row_off in key:
            lo = win_off - row_off + NA_WIN_ROWS - 1
            vals = jnp.einsum("hwd,ckd->hcwk", scaled[:, lo:lo + wr, :], sel,
                              precision=lax.Precision.HIGHEST)
            vals = jnp.where(jnp.asarray(col_in)[None, :, None, :], vals, -jnp.inf)
            pad = lambda n: jnp.full((H, GRID_W, n, GRID_W), -jnp.inf, F32)
            full = jnp.concatenate([pad(win_off), vals, pad(ww - win_off - wr)], axis=2)
            blocks.append(full.reshape(H, GRID_W, ww * GRID_W))
        pats.append(jnp.concatenate(blocks, axis=1))
    return jnp.stack(pats, axis=1)


def _na_body(q_ref, k_ref, v_ref, qg_ref, kg_ref, bias_ref, o_ref, qn_scr, kn_scr, va_scr, *, rows):
    wr, gs, ww, groups, _ = _na_plan(rows)

    def rms(x, g):
        return x * lax.rsqrt(jnp.mean(x * x, axis=-1, keepdims=True) + EPS) * g

    qn_scr[...] = (rms(q_ref[...].astype(F32), qg_ref[...]) * (NA_DIM ** -0.5 * LOG2E)).astype(BF16)
    kn_scr[...] = rms(k_ref[...].astype(F32), kg_ref[...]).astype(BF16)
    va_scr[:, :NA_DIM] = v_ref[...]
    va_scr[:, NA_DIM:] = jnp.ones((v_ref.shape[0], NA_DIM), BF16)

    for g, (ws, pid) in enumerate(groups):
        qs = slice(g * gs * GRID_W, (g + 1) * gs * GRID_W)
        ks = slice(ws * GRID_W, (ws + ww) * GRID_W)
        s = lax.dot_general(qn_scr[qs, :], kn_scr[ks, :], _NT, preferred_element_type=F32)
        s = s + bias_ref[0, pid]
        m = jnp.max(s, axis=-1, keepdims=True)
        p = jnp.exp2((s - m).astype(BF16))
        ol = jnp.dot(p, va_scr[ks, :], preferred_element_type=F32)
        o_ref[qs, :] = (ol[:, :NA_DIM] / ol[:, NA_DIM:]).astype(BF16)


def _na_attention(proj, q_gain, k_gain, bias_tab, *, batch, seq):
    T = proj.shape[0]
    rows = seq // GRID_W
    vec = pl.BlockSpec((1, NA_DIM), lambda h, b: (0, 0))
    return pl.pallas_call(
        functools.partial(_na_body, rows=rows),
        grid=(NA_HEADS, batch),
        in_specs=[pl.BlockSpec((seq, HEAD_W), lambda h, b: (b, NQ_BLK + h)),
                  pl.BlockSpec((seq, HEAD_W), lambda h, b: (b, NK_BLK + h)),
                  pl.BlockSpec((seq, HEAD_W), lambda h, b: (b, NV_BLK + h)),
                  vec, vec,
                  pl.BlockSpec((1,) + bias_tab.shape[1:], lambda h, b: (h, 0, 0, 0))],
        out_specs=pl.BlockSpec((seq, HEAD_W), lambda h, b: (b, h)),
        out_shape=jax.ShapeDtypeStruct((T, NA_HEADS * NA_DIM), BF16),
        scratch_shapes=[pltpu.VMEM((seq, HEAD_W), BF16), pltpu.VMEM((seq, HEAD_W), BF16),
                        pltpu.VMEM((seq, 2 * NA_DIM), BF16)],
        compiler_params=_cparams(("parallel", "parallel")),
        name="na_attn",
    )(proj, proj, proj, q_gain, k_gain, bias_tab)


def _outproj_body(oa_ref, ob_ref, ga_ref, gb_ref, x_ref, wd_ref, wn_ref, wo_ref, fg_ref, wr_ref,
                  x1_ref, h2_ref, lt_ref, m_scr, *, sub, ncol):
    tm, D = x_ref.shape
    for r in range(tm // sub):
        rs = slice(r * sub, (r + 1) * sub)
        oa = oa_ref[rs, :]
        ob = ob_ref[rs, :]
        for c in range(D // ncol):
            cs = slice(c * ncol, (c + 1) * ncol)
            ya = jnp.dot(oa, wd_ref[:, cs], preferred_element_type=F32)
            yb = jnp.dot(ob, wn_ref[:, cs], preferred_element_type=F32)
            m_scr[rs, cs] = (jax.nn.sigmoid(ga_ref[rs, cs].astype(F32)) * ya
                             + jax.nn.sigmoid(gb_ref[rs, cs].astype(F32)) * yb).astype(BF16)
        x1 = x_ref[rs, :] + jnp.dot(m_scr[rs, :], wo_ref[...], preferred_element_type=F32)
        x1_ref[rs, :] = x1
        h2 = x1 * lax.rsqrt(jnp.mean(x1 * x1, axis=-1, keepdims=True) + EPS) * fg_ref[...]
        h2_ref[rs, :] = h2
        h_hi = h2.astype(BF16)
        h_lo = (h2 - h_hi.astype(F32)).astype(BF16)
        both = lax.dot_general(wr_ref[...], h_hi, _NT, preferred_element_type=F32)
        cross = lax.dot_general(wr_ref[:ROUTER_ROWS, :], h_lo, _NT, preferred_element_type=F32)
        lt_ref[:, rs] = both[:ROUTER_ROWS] + both[ROUTER_ROWS:] + cross


def _outproj(oa, ob, proj, x2, wd, wn, wo, ffn_gain, wr_t, *, tm):
    T, D = x2.shape
    gate_blk = QKV_W // D
    const = lambda shape: pl.BlockSpec(shape, lambda i: (0, 0), pipeline_mode=pl.Buffered(1))
    return pl.pallas_call(
        functools.partial(_outproj_body, sub=min(tm, 128), ncol=min(D, 512)),
        grid=(T // tm,),
        in_specs=[pl.BlockSpec((tm, oa.shape[1]), lambda i: (i, 0)),
                  pl.BlockSpec((tm, ob.shape[1]), lambda i: (i, 0)),
                  pl.BlockSpec((tm, D), lambda i: (i, gate_blk)),
                  pl.BlockSpec((tm, D), lambda i: (i, gate_blk + 1)),
                  pl.BlockSpec((tm, D), lambda i: (i, 0)),
                  const(wd.shape), const(wn.shape), const(wo.shape),
                  const((1, D)), const(wr_t.shape)],
        out_specs=[pl.BlockSpec((tm, D), lambda i: (i, 0)),
                   pl.BlockSpec((tm, D), lambda i: (i, 0)),
                   pl.BlockSpec((ROUTER_ROWS, tm), lambda i: (0, i))],
        out_shape=[jax.ShapeDtypeStruct((T, D), F32),
                   jax.ShapeDtypeStruct((T, D), F32),
                   jax.ShapeDtypeStruct((ROUTER_ROWS, T), F32)],
        scratch_shapes=[pltpu.VMEM((tm, D), BF16)],
        compiler_params=_cparams(("parallel",)),
        name="outproj",
    )(oa, ob, proj, proj, x2, wd, wn, wo, ffn_gain.reshape(1, D), wr_t)


def _route_body(lt_ref, bias_ref, pos_ref, wts_ref, meta_ref, cnt_scr, carry_scr,
                *, n_groups, n_exp, tm, tc):
    ne = n_groups * n_exp
    phase = pl.program_id(0)
    step = pl.program_id(1)

    l = lt_ref[...] + bias_ref[...]
    el = l[0:ne]
    gl = l[ne:ne + n_groups]
    gmax = jnp.max(gl, axis=0, keepdims=True)
    gi = lax.broadcasted_iota(I32, gl.shape, 0)
    gsel = jnp.min(jnp.where(gl == gmax, gi, n_groups), axis=0, keepdims=True)
    gw = 1.0 / jnp.sum(jnp.exp(gl - gmax), axis=0, keepdims=True)
    ei = lax.broadcasted_iota(I32, el.shape, 0)
    lo = gsel * n_exp
    elm = jnp.where((ei >= lo) & (ei < lo + n_exp), el, -jnp.inf)
    m1 = jnp.max(elm, axis=0, keepdims=True)
    i1 = jnp.min(jnp.where(elm == m1, ei, ne), axis=0, keepdims=True)
    elm2 = jnp.where(ei == i1, -jnp.inf, elm)
    m2 = jnp.max(elm2, axis=0, keepdims=True)
    i2 = jnp.min(jnp.where(elm2 == m2, ei, ne), axis=0, keepdims=True)
    r = jnp.exp(m2 - m1)
    w1 = gw / (1.0 + r)
    w2 = w1 * r
    oh1 = ei == i1
    oh2 = ei == i2
    both = jnp.where(oh1 | oh2, 1.0, 0.0)

    @pl.when((phase == 0) & (step == 0))
    def _():
        cnt_scr[...] = jnp.zeros_like(cnt_scr)
        carry_scr[...] = jnp.zeros_like(carry_scr)

    @pl.when(phase == 0)
    def _():
        cnt_scr[...] += jnp.sum(both, axis=1, keepdims=True)

    @pl.when(phase == 1)
    def _():
        cnt = cnt_scr[...]
        tiles = jnp.floor((cnt + (tm - 1)) * (1.0 / tm))
        ea = lax.broadcasted_iota(I32, (ne, ne), 0)
        eb = lax.broadcasted_iota(I32, (ne, ne), 1)
        lower = jnp.where(eb < ea, 1.0, 0.0).astype(BF16)
        tiles_b = jnp.broadcast_to(tiles, (ne, 128)).astype(BF16)
        start_t = jnp.dot(lower, tiles_b, preferred_element_type=F32)[:, 0:1]
        ta = lax.broadcasted_iota(I32, (tc, tc), 0)
        tb = lax.broadcasted_iota(I32, (tc, tc), 1)
        upper = jnp.where(ta < tb, 1.0, 0.0).astype(BF16)
        prefix = jnp.dot(both.astype(BF16), upper, preferred_element_type=F32)
        base = prefix + carry_scr[...] + start_t * tm
        pos1 = jnp.sum(jnp.where(oh1, base, 0.0), axis=0, keepdims=True)
        pos2 = jnp.sum(jnp.where(oh2, base, 0.0), axis=0, keepdims=True)
        carry_scr[...] += jnp.sum(both, axis=1, keepdims=True)
        pos_ref[...] = jnp.zeros_like(pos_ref)
        pos_ref[0:1, :] = pos1.astype(I32)
        pos_ref[1:2, :] = pos2.astype(I32)
        wts_ref[...] = jnp.zeros_like(wts_ref)
        wts_ref[0:1, :] = w1
        wts_ref[1:2, :] = w2

        nt = meta_ref.shape[1]
        tj = lax.broadcasted_iota(I32, (ne, nt), 1).astype(F32)
        end_t = start_t + tiles
        texp = jnp.sum(jnp.where(end_t <= tj, 1.0, 0.0), axis=0, keepdims=True)
        is_first = jnp.sum(jnp.where((start_t == tj) & (tiles > 0.0), 1.0, 0.0),
                           axis=0, keepdims=True)
        is_last = jnp.sum(jnp.where((end_t - 1.0 == tj) & (tiles > 0.0), 1.0, 0.0),
                          axis=0, keepdims=True)
        nused = jnp.broadcast_to(jnp.sum(tiles, axis=0, keepdims=True), (1, nt))
        meta_ref[...] = jnp.zeros_like(meta_ref)
        meta_ref[0:1, :] = jnp.minimum(texp, ne - 1.0).astype(I32)
        meta_ref[1:2, :] = is_first.astype(I32)
        fill = is_last + jnp.where(tj[0:1, :] >= nused, 1.0, 0.0)
        meta_ref[2:3, :] = fill.astype(I32)
        meta_ref[3:4, :] = nused.astype(I32)
        used = tiles > 0.0
        seg = jnp.sum(jnp.where((end_t <= tj) & used, 1.0, 0.0), axis=0, keepdims=True)
        eidx = lax.broadcasted_iota(I32, (ne, nt), 0).astype(F32)
        nxt = jnp.min(jnp.where((start_t > tj) & used, eidx, float(ne)), axis=0, keepdims=True)
        meta_ref[4:5, :] = (seg - 2.0 * jnp.floor(seg * 0.5)).astype(I32)
        meta_ref[5:6, :] = jnp.where(nxt >= ne, -1.0, nxt).astype(I32)


def _route(lt, bias_col, *, n_groups, n_exp, tm, n_tiles, tc):
    T = lt.shape[1]
    ne = n_groups * n_exp
    nt_pad = -(-n_tiles // 128) * 128
    return pl.pallas_call(
        functools.partial(_route_body, n_groups=n_groups, n_exp=n_exp, tm=tm, tc=tc),
        grid=(2, T // tc),
        in_specs=[pl.BlockSpec((ROUTER_ROWS, tc), lambda p, i: (0, i)),
                  pl.BlockSpec((ROUTER_ROWS, 1), lambda p, i: (0, 0))],
        out_specs=[pl.BlockSpec((8, tc), lambda p, i: (0, i * p)),
                   pl.BlockSpec((8, tc), lambda p, i: (0, i * p)),
                   pl.BlockSpec((8, nt_pad), lambda p, i: (0, 0))],
        out_shape=[jax.ShapeDtypeStruct((8, T), I32),
                   jax.ShapeDtypeStruct((8, T), F32),
                   jax.ShapeDtypeStruct((8, nt_pad), I32)],
        scratch_shapes=[pltpu.VMEM((ne, 1), F32), pltpu.VMEM((ne, 1), F32)],
        compiler_params=_cparams(("arbitrary", "arbitrary")),
        name="route",
    )(lt, bias_col)


DMA_UNROLL = 8


def _dispatch_body(fill_ref, pos1_ref, pos2_ref, h_ref, xs_ref, zero_scr, sem, zsem, *, td, tm):
    n_tiles = fill_ref.shape[0]

    def zero_copy(j):
        return pltpu.make_async_copy(zero_scr, xs_ref.at[pl.ds(pl.multiple_of(j * tm, tm), tm), :], zsem)

    @pl.when(pl.program_id(0) == 0)
    def _():
        zero_scr[...] = jnp.zeros_like(zero_scr)

        def start(j, carry):
            @pl.when(fill_ref[j] == 1)
            def _():
                zero_copy(j).start()
            return carry
        lax.fori_loop(0, n_tiles, start, 0)

        def wait(j, carry):
            @pl.when(fill_ref[j] == 1)
            def _():
                zero_copy(j).wait()
            return carry
        lax.fori_loop(0, n_tiles, wait, 0)

    def row_copy(t, pos):
        return pltpu.make_async_copy(h_ref.at[pl.ds(t, 1), :], xs_ref.at[pl.ds(pos, 1), :], sem)

    def issue(t, carry):
        row_copy(t, pos1_ref[0, 0, t]).start()
        row_copy(t, pos2_ref[0, 0, t]).start()
        return carry
    lax.fori_loop(0, td, issue, 0, unroll=DMA_UNROLL)

    def drain(t, carry):
        row_copy(t, 0).wait()
        row_copy(t, 0).wait()
        return carry
    lax.fori_loop(0, td, drain, 0, unroll=DMA_UNROLL)


def _dispatch(fill, pos1, pos2, h2, *, n_rows, td, tm):
    T, W = h2.shape
    nblk = T // td
    smem3 = pl.BlockSpec((1, 1, td), lambda i, fl: (i, 0, 0), memory_space=pltpu.SMEM)
    grid_spec = pltpu.PrefetchScalarGridSpec(
        num_scalar_prefetch=1,
        grid=(nblk,),
        in_specs=[smem3, smem3, pl.BlockSpec((td, W), lambda i, fl: (i, 0))],
        out_specs=pl.BlockSpec(memory_space=pl.ANY),
        scratch_shapes=[pltpu.VMEM((tm, W), h2.dtype), pltpu.SemaphoreType.DMA(()),
                        pltpu.SemaphoreType.DMA(())],
    )
    return pl.pallas_call(
        functools.partial(_dispatch_body, td=td, tm=tm),
        grid_spec=grid_spec,
        out_shape=jax.ShapeDtypeStruct((n_rows, W), h2.dtype),
        compiler_params=_cparams(("arbitrary",)),
        name="dispatch",
    )(fill, pos1.reshape(nblk, 1, td), pos2.reshape(nblk, 1, td), h2)


def _expert_weights(plan, j, hbm_refs, buf_refs, bf16_refs, sem):
    texp_ref, first_ref, slot_ref, next_ref = plan

    def copies(e, slot):
        return [pltpu.make_async_copy(h.at[e], b.at[slot], sem.at[i, slot])
                for i, (h, b) in enumerate(zip(hbm_refs, buf_refs))]

    @pl.when(j == 0)
    def _():
        for c in copies(texp_ref[0], 0):
            c.start()

    @pl.when(first_ref[j] == 1)
    def _():
        slot = slot_ref[j]
        for c in copies(texp_ref[j], slot):
            c.wait()

        @pl.when(next_ref[j] >= 0)
        def _():
            for c in copies(next_ref[j], 1 - slot):
                c.start()
        for b, w in zip(buf_refs, bf16_refs):
            w[...] = b[slot].astype(BF16)


def _gateup_body(texp_ref, first_ref, nused_ref, slot_ref, next_ref, x_ref, wg_hbm, wu_hbm, hid_ref,
                 wg_buf, wu_buf, wg_scr, wu_scr, sem):
    j = pl.program_id(0)

    @pl.when(j < nused_ref[0])
    def _():
        _expert_weights((texp_ref, first_ref, slot_ref, next_ref), j, (wg_hbm, wu_hbm),
                        (wg_buf, wu_buf), (wg_scr, wu_scr), sem)
        x = x_ref[...].astype(BF16)
        g = jnp.dot(x, wg_scr[...], preferred_element_type=F32)
        u = jnp.dot(x, wu_scr[...], preferred_element_type=F32)
        hid_ref[...] = (g * jax.nn.sigmoid(g) * u).astype(BF16)

    @pl.when(j >= nused_ref[0])
    def _():
        hid_ref[...] = jnp.zeros_like(hid_ref)


def _used_tile(j, nused):
    return jnp.minimum(j, nused[0] - 1)


def _gateup(plan, xs, wg, wu, *, tm):
    P, W = xs.shape
    D, F = wg.shape[1:]
    hbm = pl.BlockSpec(memory_space=pl.ANY)
    grid_spec = pltpu.PrefetchScalarGridSpec(
        num_scalar_prefetch=5,
        grid=(P // tm,),
        in_specs=[pl.BlockSpec((tm, W), lambda j, te, fi, nu, sl, nx: (_used_tile(j, nu), 0)), hbm, hbm],
        out_specs=pl.BlockSpec((tm, F), lambda j, te, fi, nu, sl, nx: (j, 0)),
        scratch_shapes=[pltpu.VMEM((2, D, F), wg.dtype), pltpu.VMEM((2, D, F), wu.dtype),
                        pltpu.VMEM((D, F), BF16), pltpu.VMEM((D, F), BF16),
                        pltpu.SemaphoreType.DMA((2, 2))],
    )
    return pl.pallas_call(
        _gateup_body,
        grid_spec=grid_spec,
        out_shape=jax.ShapeDtypeStruct((P, F), BF16),
        compiler_params=_cparams(("arbitrary",)),
        name="expert_gateup",
    )(*plan, xs, wg, wu)


def _down_body(texp_ref, first_ref, nused_ref, slot_ref, next_ref, hid_ref, wd_hbm, y_ref,
               wd_buf, wd_scr, sem):
    j = pl.program_id(0)

    @pl.when(j < nused_ref[0])
    def _():
        _expert_weights((texp_ref, first_ref, slot_ref, next_ref), j, (wd_hbm,), (wd_buf,),
                        (wd_scr,), sem)
        y_ref[...] = jnp.dot(hid_ref[...], wd_scr[...], preferred_element_type=F32)

    @pl.when(j >= nused_ref[0])
    def _():
        y_ref[...] = jnp.zeros_like(y_ref)


def _down(plan, hid, wd, *, tm):
    P, F = hid.shape
    D = wd.shape[2]
    grid_spec = pltpu.PrefetchScalarGridSpec(
        num_scalar_prefetch=5,
        grid=(P // tm,),
        in_specs=[pl.BlockSpec((tm, F), lambda j, te, fi, nu, sl, nx: (_used_tile(j, nu), 0)),
                  pl.BlockSpec(memory_space=pl.ANY)],
        out_specs=pl.BlockSpec((tm, D), lambda j, te, fi, nu, sl, nx: (j, 0)),
        scratch_shapes=[pltpu.VMEM((2, F, D), wd.dtype), pltpu.VMEM((F, D), BF16),
                        pltpu.SemaphoreType.DMA((1, 2))],
    )
    return pl.pallas_call(
        _down_body,
        grid_spec=grid_spec,
        out_shape=jax.ShapeDtypeStruct((P, D), F32),
        compiler_params=_cparams(("arbitrary",)),
        name="expert_down",
    )(*plan, hid, wd)


def _combine_body(pos1_ref, pos2_ref, ys_ref, x1_ref, w1_ref, w2_ref, o_ref, r1_scr, r2_scr, sem, *, tc):
    def row_copy(pos, dst, t):
        return pltpu.make_async_copy(ys_ref.at[pl.ds(pos, 1), :], dst.at[pl.ds(t, 1), :], sem)

    def issue(t, carry):
        row_copy(pos1_ref[0, 0, t], r1_scr, t).start()
        row_copy(pos2_ref[0, 0, t], r2_scr, t).start()
        return carry
    lax.fori_loop(0, tc, issue, 0, unroll=DMA_UNROLL)

    def drain(t, carry):
        row_copy(0, r1_scr, t).wait()
        row_copy(0, r2_scr, t).wait()
        return carry
    lax.fori_loop(0, tc, drain, 0, unroll=DMA_UNROLL)

    o_ref[...] = x1_ref[...] + w1_ref[...] * r1_scr[...] + w2_ref[...] * r2_scr[...]


def _combine(pos1, pos2, ys, x1, w1, w2, *, tc):
    T, D = x1.shape
    nblk = T // tc
    smem3 = pl.BlockSpec((1, 1, tc), lambda i: (i, 0, 0), memory_space=pltpu.SMEM)
    col = pl.BlockSpec((tc, 1), lambda i: (i, 0))
    return pl.pallas_call(
        functools.partial(_combine_body, tc=tc),
        grid=(nblk,),
        in_specs=[smem3, smem3, pl.BlockSpec(memory_space=pl.ANY),
                  pl.BlockSpec((tc, D), lambda i: (i, 0)), col, col],
        out_specs=pl.BlockSpec((tc, D), lambda i: (i, 0)),
        out_shape=jax.ShapeDtypeStruct((T, D), F32),
        scratch_shapes=[pltpu.VMEM((tc, D), F32), pltpu.VMEM((tc, D), F32),
                        pltpu.SemaphoreType.DMA(())],
        compiler_params=_cparams(("arbitrary",)),
        name="combine",
    )(pos1.reshape(nblk, 1, tc), pos2.reshape(nblk, 1, tc), ys, x1, w1.reshape(T, 1), w2.reshape(T, 1))


def _tile_sizes(T, D, seq, in_w):
    pick = lambda n, pref: pref if n % pref == 0 else n
    return dict(
        inproj_tm=pick(T, 1024), inproj_tn=next(t for t in (1024, 512, 256, 128) if in_w % t == 0),
        diff_tq=pick(seq, 256),
        outproj_tm=pick(T, 256),
        route_tc=pick(T, 512),
        expert_tm=256,
        dispatch_td=pick(T, 256), combine_tc=pick(T, 256),
    )


def _layer(x2, p, l, *, batch, seq):
    T, D = x2.shape
    ts = _tile_sizes(T, D, seq, p["w_in"].shape[-1])
    lambda_init = 0.8 - 0.6 * math.exp(-0.3 * l)
    n_groups, n_exp = p["b_router_expert"].shape[1:]
    ne = n_groups * n_exp
    rows = seq // GRID_W

    proj = _inproj(x2, p["mix_norm"][l], p["w_in"][l].astype(BF16),
                   tm=ts["inproj_tm"], tn=ts["inproj_tn"])

    lam = p["diff_lambda"][l].astype(F32)
    lam_full = (jnp.exp(jnp.sum(lam[0] * lam[1])) - jnp.exp(jnp.sum(lam[2] * lam[3]))
                + lambda_init).reshape(1)
    slopes = 2.0 ** (-8.0 * jnp.arange(1, DIFF_HEADS + 1, dtype=F32) / DIFF_HEADS)
    two = lambda g: jnp.tile(g.astype(F32), 2).reshape(1, HEAD_W)
    oa = _diff_attention(proj, _alibi_tiles(slopes, seq), lam_full, two(p["diff_q_norm"][l]), two(p["diff_k_norm"][l]),
                         p["diff_subln"][l].astype(F32).reshape(1, HEAD_W),
                         batch=batch, seq=seq, tq=ts["diff_tq"], out_scale=1.0 - lambda_init)

    ob = _na_attention(proj, p["na_q_norm"][l].astype(F32).reshape(1, NA_DIM),
                       p["na_k_norm"][l].astype(F32).reshape(1, NA_DIM),
                       _na_bias_table(p["na_rpb"][l], rows), batch=batch, seq=seq)

    wr_t = jnp.concatenate([p["w_router_expert"][l].T, p["w_router_group"][l].T,
                            jnp.zeros((ROUTER_ROWS - ne - n_groups, D), F32)], axis=0).astype(F32)
    wr_hi = wr_t.astype(BF16)
    wr_t = jnp.concatenate([wr_hi, (wr_t - wr_hi.astype(F32)).astype(BF16)], axis=0)
    b_col = jnp.concatenate([p["b_router_expert"][l].reshape(ne), p["b_router_group"][l],
                             jnp.zeros((ROUTER_ROWS - ne - n_groups,), F32)]).astype(F32).reshape(ROUTER_ROWS, 1)
    x1, h2, lt = _outproj(oa, ob, proj, x2, p["w_diff_out"][l].astype(BF16),
                          p["w_na_out"][l].astype(BF16), p["w_out"][l].astype(BF16),
                          p["ffn_norm"][l].astype(F32), wr_t, tm=ts["outproj_tm"])

    tm = ts["expert_tm"]
    n_tiles = (T * TOP_K) // tm + ne
    pos, wts, meta = _route(lt, b_col, n_groups=n_groups, n_exp=n_exp, tm=tm, n_tiles=n_tiles,
                            tc=ts["route_tc"])
    fill = meta[2, :n_tiles]
    plan = (meta[0, :n_tiles], meta[1, :n_tiles], meta[3, :1], meta[4, :n_tiles], meta[5, :n_tiles])

    xs = _dispatch(fill, pos[0], pos[1], h2, n_rows=n_tiles * tm, td=ts["dispatch_td"], tm=tm)
    F = p["w_expert_gate"].shape[-1]
    hid = _gateup(plan, xs, p["w_expert_gate"][l].reshape(ne, D, F),
                  p["w_expert_up"][l].reshape(ne, D, F), tm=tm)
    ys = _down(plan, hid, p["w_expert_down"][l].reshape(ne, F, D), tm=tm)
    return _combine(pos[0], pos[1], ys, x1, wts[0], wts[1], tc=ts["combine_tc"])


def kernel(x, mix_norm, w_in, diff_q_norm, diff_k_norm, diff_lambda, diff_subln, na_q_norm, na_k_norm, na_rpb, w_diff_out, w_na_out, w_out, ffn_norm, w_router_group, b_router_group, w_router_expert, b_router_expert, w_expert_gate, w_expert_up, w_expert_down):
    B, S, D = x.shape
    assert S % GRID_W == 0 and S % 128 == 0 and QKV_W % D == 0
    p = dict(mix_norm=mix_norm, w_in=w_in, diff_q_norm=diff_q_norm, diff_k_norm=diff_k_norm,
             diff_lambda=diff_lambda, diff_subln=diff_subln, na_q_norm=na_q_norm,
             na_k_norm=na_k_norm, na_rpb=na_rpb, w_diff_out=w_diff_out, w_na_out=w_na_out,
             w_out=w_out, ffn_norm=ffn_norm, w_router_group=w_router_group,
             b_router_group=b_router_group, w_router_expert=w_router_expert,
             b_router_expert=b_router_expert, w_expert_gate=w_expert_gate,
             w_expert_up=w_expert_up, w_expert_down=w_expert_down)
    x2 = x.reshape(B * S, D).astype(F32)
    for l in range(mix_norm.shape[0]):
        x2 = _layer(x2, p, l, batch=B, seq=S)
    return x2.reshape(B, S, D).astype(x.dtype)
```

```python
import functools
import math

import jax
import jax.numpy as jnp
import numpy as np
from jax import lax
from jax.experimental import pallas as pl
from jax.experimental.pallas import tpu as pltpu

F32 = jnp.float32
BF16 = jnp.bfloat16
I32 = jnp.int32

EPS = 1e-6
LOG2E = math.log2(math.e)
GRID_W = 64
DIFF_HEADS = 8
DIFF_QK_DIM = 64
NA_HEADS = 8
NA_DIM = 128
NA_WIN_ROWS = 8
NA_WIN_COLS = 16
TOP_K = 2
HEAD_W = 128
DQ_BLK, DK_BLK, DV_BLK = 0, DIFF_HEADS, 2 * DIFF_HEADS
NQ_BLK = 3 * DIFF_HEADS
NK_BLK = NQ_BLK + NA_HEADS
NV_BLK = NQ_BLK + 2 * NA_HEADS
QKV_W = (3 * DIFF_HEADS + 3 * NA_HEADS) * HEAD_W

ROUTER_ROWS = 48
VMEM_LIMIT = 56 * 1024 * 1024

_NT = (((1,), (1,)), ((), ()))


def _cparams(sem, vmem=VMEM_LIMIT):
    return pltpu.CompilerParams(dimension_semantics=sem, vmem_limit_bytes=vmem)


def _inproj_body(x_ref, g_ref, w_ref, o_ref, h_scr, *, rows):
    @pl.when(pl.program_id(1) == 0)
    def _():
        def chunk(c, carry):
            r = pl.ds(pl.multiple_of(c * rows, rows), rows)
            x = x_ref[r, :]
            ms = jnp.mean(x * x, axis=-1, keepdims=True)
            h_scr[r, :] = (x * lax.rsqrt(ms + EPS) * g_ref[...]).astype(BF16)
            return carry
        lax.fori_loop(0, x_ref.shape[0] // rows, chunk, 0)

    o_ref[...] = jnp.dot(h_scr[...], w_ref[...], preferred_element_type=F32).astype(BF16)


def _inproj(x2, gain, w_bf16, *, tm, tn):
    T, D = x2.shape
    N = w_bf16.shape[1]
    return pl.pallas_call(
        functools.partial(_inproj_body, rows=128),
        grid=(T // tm, N // tn),
        in_specs=[pl.BlockSpec((tm, D), lambda i, j: (i, 0)),
                  pl.BlockSpec((1, D), lambda i, j: (0, 0)),
                  pl.BlockSpec((D, tn), lambda i, j: (0, j))],
        out_specs=pl.BlockSpec((tm, tn), lambda i, j: (i, j)),
        out_shape=jax.ShapeDtypeStruct((T, N), BF16),
        scratch_shapes=[pltpu.VMEM((tm, D), BF16)],
        compiler_params=_cparams(("parallel", "arbitrary")),
        name="inproj",
    )(x2, gain.reshape(1, D), w_bf16)


def _half_rms(x, gain, first):
    x2 = x * x
    s0 = jnp.sum(jnp.where(first, x2, 0.0), axis=-1, keepdims=True)
    s1 = jnp.sum(jnp.where(first, 0.0, x2), axis=-1, keepdims=True)
    inv = jnp.where(first, lax.rsqrt(s0 * (1.0 / DIFF_QK_DIM) + EPS),
                    lax.rsqrt(s1 * (1.0 / DIFF_QK_DIM) + EPS))
    return x * inv * gain


def _alibi_tiles(slopes, seq):
    nb = seq // HEAD_W
    off = np.arange(HEAD_W)
    delta = (np.arange(-(nb - 1), nb)[:, None, None] * HEAD_W + off[None, None, :] - off[None, :, None])
    dist = jnp.asarray(np.abs(delta).astype(np.float32))
    return (-LOG2E * slopes.astype(F32))[:, None, None, None] * dist


def _diff_body(lam_ref, q_ref, k_ref, v_ref, qg_ref, kg_ref, sg_ref, bias_ref, o_ref,
               qm_scr, kn_scr, vt_scr, s0_scr, s1_scr, p_scr, ot_scr, *, tq, seq, out_scale):
    nkb = seq // HEAD_W
    nq = seq // tq
    qblks = tq // HEAD_W
    groups = HEAD_W // 8
    lane = lax.broadcasted_iota(I32, (1, HEAD_W), 1)
    first = lane < DIFF_QK_DIM
    kn_scr[...] = _half_rms(k_ref[...].astype(F32), kg_ref[...], first).astype(BF16)
    vt_scr[:HEAD_W, :] = v_ref[...].astype(F32).T.astype(BF16)
    vt_scr[HEAD_W:, :] = jnp.ones((8, seq), BF16)
    q = _half_rms(q_ref[...].astype(F32), qg_ref[...], first) * (DIFF_QK_DIM ** -0.5 * LOG2E)
    qm_scr[0] = jnp.where(first, q, 0.0).astype(BF16)
    qm_scr[1] = jnp.where(first, 0.0, q).astype(BF16)
    lam = lam_ref[0]

    def stage(score, expo):
        mx = [jnp.full((8, tq), -jnp.inf, F32), jnp.full((8, tq), -jnp.inf, F32)]
        for c in range(nkb):
            ks = slice(c * HEAD_W, (c + 1) * HEAD_W)
            if score is not None:
                qa, dst = score
                rows = pl.ds(pl.multiple_of(qa * tq, tq), tq)
                kc = kn_scr[ks, :]
                tile0 = qa * qblks - c + (nkb - 1)
                bias = jnp.concatenate([bias_ref[0, tile0 + t] for t in range(qblks)], axis=1)
                for m in range(2):
                    s = lax.dot_general(kc, qm_scr[m, rows, :], _NT,
                                        preferred_element_type=F32) + bias
                    dst[m, ks, :] = s
                    mx[m] = jnp.maximum(mx[m], jnp.max(s.reshape(groups, 8, tq), axis=0))
            if expo is not None:
                src, mrow = expo
                for m in range(2):
                    p_scr[m, ks, :] = jnp.exp2((src[m, ks, :] - mrow[m]).astype(BF16))
        return tuple(jnp.max(a, axis=0, keepdims=True) for a in mx)

    def values(qb):
        outs = []
        for m in range(2):
            ol = jnp.dot(vt_scr[...], p_scr[m], preferred_element_type=F32)
            outs.append(ol[:HEAD_W] / ol[HEAD_W:HEAD_W + 1])
        ot_scr[qb] = outs[0] - lam * outs[1]

    mrow = stage((0, s0_scr), None)

    def pair(j, mrow):
        mrow1 = stage((2 * j + 1, s1_scr), (s0_scr, mrow))
        values(2 * j)
        mrow0 = stage((2 * j + 2, s0_scr), (s1_scr, mrow1))
        values(2 * j + 1)
        return mrow0
    mrow = lax.fori_loop(0, nq // 2 - 1, pair, mrow)
    mrow1 = stage((nq - 1, s1_scr), (s0_scr, mrow))
    values(nq - 2)
    stage(None, (s1_scr, mrow1))
    values(nq - 1)

    gain = sg_ref[...] * out_scale
    for i in range(nq):
        o = ot_scr[i]
        o = o * lax.rsqrt(jnp.mean(o * o, axis=0, keepdims=True) + EPS)
        o_ref[i * tq:(i + 1) * tq, :] = (o.T * gain).astype(BF16)


def _diff_attention(proj, bias_tiles, lam, q_gain, k_gain, sub_gain, *, batch, seq, tq, out_scale):
    T = proj.shape[0]
    smem = pl.BlockSpec(memory_space=pltpu.SMEM)
    vec = pl.BlockSpec((1, HEAD_W), lambda h, b: (0, 0))
    blk = lambda col: pl.BlockSpec((seq, HEAD_W), lambda h, b: (b, col + h))
    return pl.pallas_call(
        functools.partial(_diff_body, tq=tq, seq=seq, out_scale=out_scale),
        grid=(DIFF_HEADS, batch),
        in_specs=[smem, blk(DQ_BLK), blk(DK_BLK), blk(DV_BLK), vec, vec, vec,
                  pl.BlockSpec((1,) + bias_tiles.shape[1:], lambda h, b: (h, 0, 0, 0))],
        out_specs=pl.BlockSpec((seq, HEAD_W), lambda h, b: (b, h)),
        out_shape=jax.ShapeDtypeStruct((T, DIFF_HEADS * HEAD_W), BF16),
        scratch_shapes=[pltpu.VMEM((2, seq, HEAD_W), BF16), pltpu.VMEM((seq, HEAD_W), BF16),
                        pltpu.VMEM((HEAD_W + 8, seq), BF16),
                        pltpu.VMEM((2, seq, tq), F32), pltpu.VMEM((2, seq, tq), F32),
                        pltpu.VMEM((2, seq, tq), BF16), pltpu.VMEM((seq // tq, HEAD_W, tq), F32)],
        compiler_params=_cparams(("parallel", "parallel")),
        name="diff_attn",
    )(lam, proj, proj, proj, q_gain, k_gain, sub_gain, bias_tiles)


NA_GROUP_ROWS = 4


def _na_plan(rows):
    wr = min(NA_WIN_ROWS, rows)
    gs = NA_GROUP_ROWS if rows % NA_GROUP_ROWS == 0 else 1
    ww = wr + gs - 1
    ww = min(ww + ww % 2, rows)
    row_start = lambda r: min(max(r - wr // 2, 0), rows - wr)
    groups, patterns = [], {}
    for g in range(rows // gs):
        ws = min(row_start(g * gs), rows - ww)
        key = tuple((row_start(r) - ws, r - ws) for r in range(g * gs, (g + 1) * gs))
        groups.append((ws, patterns.setdefault(key, len(patterns))))
    return wr, gs, ww, groups, list(patterns)


def _na_bias_table(rpb, rows):
    wr, gs, ww, _, patterns = _na_plan(rows)
    H = rpb.shape[0]
    c = np.arange(GRID_W)
    col_start = np.clip(c - NA_WIN_COLS // 2, 0, GRID_W - NA_WIN_COLS)
    col_in = (c[None, :] >= col_start[:, None]) & (c[None, :] < col_start[:, None] + NA_WIN_COLS)
    dc = np.clip(c[None, :] - c[:, None], -(NA_WIN_COLS - 1), NA_WIN_COLS - 1) + (NA_WIN_COLS - 1)
    sel = jnp.asarray((dc[:, :, None] == np.arange(2 * NA_WIN_COLS - 1)[None, None, :]).astype(np.float32))
    scaled = rpb.astype(F32) * LOG2E
    pats = []
    for key in patterns:
        blocks = []
        for win_off, row_off in key:
            lo = win_off - row_off + NA_WIN_ROWS - 1
            vals = jnp.einsum("hwd,ckd->hcwk", scaled[:, lo:lo + wr, :], sel,
                              precision=lax.Precision.HIGHEST)
            vals = jnp.where(jnp.asarray(col_in)[None, :, None, :], vals, -jnp.inf)
            pad = lambda n: jnp.full((H, GRID_W, n, GRID_W), -jnp.inf, F32)
            full = jnp.concatenate([pad(win_off), vals, pad(ww - win_off - wr)], axis=2)
            blocks.append(full.reshape(H, GRID_W, ww * GRID_W))
        pats.append(jnp.concatenate(blocks, axis=1))
    return jnp.stack(pats, axis=1)


def _na_body(q_ref, k_ref, v_ref, qg_ref, kg_ref, bias_ref, o_ref, qn_scr, kn_scr, va_scr, *, rows):
    wr, gs, ww, groups, _ = _na_plan(rows)

    def rms(x, g):
        return x * lax.rsqrt(jnp.mean(x * x, axis=-1, keepdims=True) + EPS) * g

    qn_scr[...] = (rms(q_ref[...].astype(F32), qg_ref[...]) * (NA_DIM ** -0.5 * LOG2E)).astype(BF16)
    kn_scr[...] = rms(k_ref[...].astype(F32), kg_ref[...]).astype(BF16)
    va_scr[:, :NA_DIM] = v_ref[...]
    va_scr[:, NA_DIM:] = jnp.ones((v_ref.shape[0], NA_DIM), BF16)

    for g, (ws, pid) in enumerate(groups):
        qs = slice(g * gs * GRID_W, (g + 1) * gs * GRID_W)
        ks = slice(ws * GRID_W, (ws + ww) * GRID_W)
        s = lax.dot_general(qn_scr[qs, :], kn_scr[ks, :], _NT, preferred_element_type=F32)
        s = s + bias_ref[0, pid]
        m = jnp.max(s, axis=-1, keepdims=True)
        p = jnp.exp2((s - m).astype(BF16))
        ol = jnp.dot(p, va_scr[ks, :], preferred_element_type=F32)
        o_ref[qs, :] = (ol[:, :NA_DIM] / ol[:, NA_DIM:]).astype(BF16)


def _na_attention(proj, q_gain, k_gain, bias_tab, *, batch, seq):
    T = proj.shape[0]
    rows = seq // GRID_W
    vec = pl.BlockSpec((1, NA_DIM), lambda h, b: (0, 0))
    return pl.pallas_call(
        functools.partial(_na_body, rows=rows),
        grid=(NA_HEADS, batch),
        in_specs=[pl.BlockSpec((seq, HEAD_W), lambda h, b: (b, NQ_BLK + h)),
                  pl.BlockSpec((seq, HEAD_W), lambda h, b: (b, NK_BLK + h)),
                  pl.BlockSpec((seq, HEAD_W), lambda h, b: (b, NV_BLK + h)),
                  vec, vec,
                  pl.BlockSpec((1,) + bias_tab.shape[1:], lambda h, b: (h, 0, 0, 0))],
        out_specs=pl.BlockSpec((seq, HEAD_W), lambda h, b: (b, h)),
        out_shape=jax.ShapeDtypeStruct((T, NA_HEADS * NA_DIM), BF16),
        scratch_shapes=[pltpu.VMEM((seq, HEAD_W), BF16), pltpu.VMEM((seq, HEAD_W), BF16),
                        pltpu.VMEM((seq, 2 * NA_DIM), BF16)],
        compiler_params=_cparams(("parallel", "parallel")),
        name="na_attn",
    )(proj, proj, proj, q_gain, k_gain, bias_tab)


def _outproj_body(oa_ref, ob_ref, ga_ref, gb_ref, x_ref, wd_ref, wn_ref, wo_ref, fg_ref, wr_ref,
                  x1_ref, lt_ref, m_scr, *, sub, ncol):
    tm, D = x_ref.shape
    for r in range(tm // sub):
        rs = slice(r * sub, (r + 1) * sub)
        oa = oa_ref[rs, :]
        ob = ob_ref[rs, :]
        for c in range(D // ncol):
            cs = slice(c * ncol, (c + 1) * ncol)
            ya = jnp.dot(oa, wd_ref[:, cs], preferred_element_type=F32)
            yb = jnp.dot(ob, wn_ref[:, cs], preferred_element_type=F32)
            m_scr[rs, cs] = (jax.nn.sigmoid(ga_ref[rs, cs].astype(F32)) * ya
                             + jax.nn.sigmoid(gb_ref[rs, cs].astype(F32)) * yb).astype(BF16)
        x1 = x_ref[rs, :] + jnp.dot(m_scr[rs, :], wo_ref[...], preferred_element_type=F32)
        x1_ref[rs, :] = x1
        h2 = x1 * lax.rsqrt(jnp.mean(x1 * x1, axis=-1, keepdims=True) + EPS) * fg_ref[...]
        h_hi = h2.astype(BF16)
        h_lo = (h2 - h_hi.astype(F32)).astype(BF16)
        both = lax.dot_general(wr_ref[...], h_hi, _NT, preferred_element_type=F32)
        cross = lax.dot_general(wr_ref[:ROUTER_ROWS, :], h_lo, _NT, preferred_element_type=F32)
        lt_ref[:, rs] = both[:ROUTER_ROWS] + both[ROUTER_ROWS:] + cross


def _outproj(oa, ob, proj, x2, wd, wn, wo, ffn_gain, wr_t, *, tm):
    T, D = x2.shape
    gate_blk = QKV_W // D
    const = lambda shape: pl.BlockSpec(shape, lambda i: (0, 0), pipeline_mode=pl.Buffered(1))
    return pl.pallas_call(
        functools.partial(_outproj_body, sub=min(tm, 128), ncol=min(D, 512)),
        grid=(T // tm,),
        in_specs=[pl.BlockSpec((tm, oa.shape[1]), lambda i: (i, 0)),
                  pl.BlockSpec((tm, ob.shape[1]), lambda i: (i, 0)),
                  pl.BlockSpec((tm, D), lambda i: (i, gate_blk)),
                  pl.BlockSpec((tm, D), lambda i: (i, gate_blk + 1)),
                  pl.BlockSpec((tm, D), lambda i: (i, 0)),
                  const(wd.shape), const(wn.shape), const(wo.shape),
                  const((1, D)), const(wr_t.shape)],
        out_specs=[pl.BlockSpec((tm, D), lambda i: (i, 0)),
                   pl.BlockSpec((ROUTER_ROWS, tm), lambda i: (0, i))],
        out_shape=[jax.ShapeDtypeStruct((T, D), F32),
                   jax.ShapeDtypeStruct((ROUTER_ROWS, T), F32)],
        scratch_shapes=[pltpu.VMEM((tm, D), BF16)],
        compiler_params=_cparams(("parallel",)),
        name="outproj",
    )(oa, ob, proj, proj, x2, wd, wn, wo, ffn_gain.reshape(1, D), wr_t)


def _route_body(lt_ref, bias_ref, pos_ref, wts_ref, meta_ref, cnt_scr, carry_scr,
                *, n_groups, n_exp, tm, tc):
    ne = n_groups * n_exp
    phase = pl.program_id(0)
    step = pl.program_id(1)

    l = lt_ref[...] + bias_ref[...]
    el = l[0:ne]
    gl = l[ne:ne + n_groups]
    gmax = jnp.max(gl, axis=0, keepdims=True)
    gi = lax.broadcasted_iota(I32, gl.shape, 0)
    gsel = jnp.min(jnp.where(gl == gmax, gi, n_groups), axis=0, keepdims=True)
    gw = 1.0 / jnp.sum(jnp.exp(gl - gmax), axis=0, keepdims=True)
    ei = lax.broadcasted_iota(I32, el.shape, 0)
    lo = gsel * n_exp
    elm = jnp.where((ei >= lo) & (ei < lo + n_exp), el, -jnp.inf)
    m1 = jnp.max(elm, axis=0, keepdims=True)
    i1 = jnp.min(jnp.where(elm == m1, ei, ne), axis=0, keepdims=True)
    elm2 = jnp.where(ei == i1, -jnp.inf, elm)
    m2 = jnp.max(elm2, axis=0, keepdims=True)
    i2 = jnp.min(jnp.where(elm2 == m2, ei, ne), axis=0, keepdims=True)
    r = jnp.exp(m2 - m1)
    w1 = gw / (1.0 + r)
    w2 = w1 * r
    oh1 = ei == i1
    oh2 = ei == i2
    both = jnp.where(oh1 | oh2, 1.0, 0.0)

    @pl.when((phase == 0) & (step == 0))
    def _():
        cnt_scr[...] = jnp.zeros_like(cnt_scr)
        carry_scr[...] = jnp.zeros_like(carry_scr)

    @pl.when(phase == 0)
    def _():
        cnt_scr[...] += jnp.sum(both, axis=1, keepdims=True)

    @pl.when(phase == 1)
    def _():
        cnt = cnt_scr[...]
        tiles = jnp.floor((cnt + (tm - 1)) * (1.0 / tm))
        ea = lax.broadcasted_iota(I32, (ne, ne), 0)
        eb = lax.broadcasted_iota(I32, (ne, ne), 1)
        lower = jnp.where(eb < ea, 1.0, 0.0).astype(BF16)
        tiles_b = jnp.broadcast_to(tiles, (ne, 128)).astype(BF16)
        start_t = jnp.dot(lower, tiles_b, preferred_element_type=F32)[:, 0:1]
        ta = lax.broadcasted_iota(I32, (tc, tc), 0)
        tb = lax.broadcasted_iota(I32, (tc, tc), 1)
        upper = jnp.where(ta < tb, 1.0, 0.0).astype(BF16)
        prefix = jnp.dot(both.astype(BF16), upper, preferred_element_type=F32)
        base = prefix + carry_scr[...] + start_t * tm
        pos1 = jnp.sum(jnp.where(oh1, base, 0.0), axis=0, keepdims=True)
        pos2 = jnp.sum(jnp.where(oh2, base, 0.0), axis=0, keepdims=True)
        carry_scr[...] += jnp.sum(both, axis=1, keepdims=True)
        pos_ref[...] = jnp.zeros_like(pos_ref)
        pos_ref[0:1, :] = pos1.astype(I32)
        pos_ref[1:2, :] = pos2.astype(I32)
        wts_ref[...] = jnp.zeros_like(wts_ref)
        wts_ref[0:1, :] = w1
        wts_ref[1:2, :] = w2

        nt = meta_ref.shape[1]
        tj = lax.broadcasted_iota(I32, (ne, nt), 1).astype(F32)
        end_t = start_t + tiles
        texp = jnp.sum(jnp.where(end_t <= tj, 1.0, 0.0), axis=0, keepdims=True)
        is_first = jnp.sum(jnp.where((start_t == tj) & (tiles > 0.0), 1.0, 0.0),
                           axis=0, keepdims=True)
        is_last = jnp.sum(jnp.where((end_t - 1.0 == tj) & (tiles > 0.0), 1.0, 0.0),
                          axis=0, keepdims=True)
        nused = jnp.broadcast_to(jnp.sum(tiles, axis=0, keepdims=True), (1, nt))
        meta_ref[...] = jnp.zeros_like(meta_ref)
        meta_ref[0:1, :] = jnp.minimum(texp, ne - 1.0).astype(I32)
        meta_ref[1:2, :] = is_first.astype(I32)
        fill = is_last + jnp.where(tj[0:1, :] >= nused, 1.0, 0.0)
        meta_ref[2:3, :] = fill.astype(I32)
        meta_ref[3:4, :] = nused.astype(I32)
        used = tiles > 0.0
        seg = jnp.sum(jnp.where((end_t <= tj) & used, 1.0, 0.0), axis=0, keepdims=True)
        eidx = lax.broadcasted_iota(I32, (ne, nt), 0).astype(F32)
        nxt = jnp.min(jnp.where((start_t > tj) & used, eidx, float(ne)), axis=0, keepdims=True)
        meta_ref[4:5, :] = (seg - 2.0 * jnp.floor(seg * 0.5)).astype(I32)
        meta_ref[5:6, :] = jnp.where(nxt >= ne, -1.0, nxt).astype(I32)


def _route(lt, bias_col, *, n_groups, n_exp, tm, n_tiles, tc):
    T = lt.shape[1]
    ne = n_groups * n_exp
    nt_pad = -(-n_tiles // 128) * 128
    return pl.pallas_call(
        functools.partial(_route_body, n_groups=n_groups, n_exp=n_exp, tm=tm, tc=tc),
        grid=(2, T // tc),
        in_specs=[pl.BlockSpec((ROUTER_ROWS, tc), lambda p, i: (0, i)),
                  pl.BlockSpec((ROUTER_ROWS, 1), lambda p, i: (0, 0))],
        out_specs=[pl.BlockSpec((8, tc), lambda p, i: (0, i * p)),
                   pl.BlockSpec((8, tc), lambda p, i: (0, i * p)),
                   pl.BlockSpec((8, nt_pad), lambda p, i: (0, 0))],
        out_shape=[jax.ShapeDtypeStruct((8, T), I32),
                   jax.ShapeDtypeStruct((8, T), F32),
                   jax.ShapeDtypeStruct((8, nt_pad), I32)],
        scratch_shapes=[pltpu.VMEM((ne, 1), F32), pltpu.VMEM((ne, 1), F32)],
        compiler_params=_cparams(("arbitrary", "arbitrary")),
        name="route",
    )(lt, bias_col)


DMA_UNROLL = 8


def _dispatch_body(fill_ref, pos1_ref, pos2_ref, h_ref, xs_ref, zero_scr, sem, zsem, *, td, tm):
    n_tiles = fill_ref.shape[0]

    def zero_copy(j):
        return pltpu.make_async_copy(zero_scr, xs_ref.at[pl.ds(pl.multiple_of(j * tm, tm), tm), :], zsem)

    @pl.when(pl.program_id(0) == 0)
    def _():
        zero_scr[...] = jnp.zeros_like(zero_scr)

        def start(j, carry):
            @pl.when(fill_ref[j] == 1)
            def _():
                zero_copy(j).start()
            return carry
        lax.fori_loop(0, n_tiles, start, 0)

        def wait(j, carry):
            @pl.when(fill_ref[j] == 1)
            def _():
                zero_copy(j).wait()
            return carry
        lax.fori_loop(0, n_tiles, wait, 0)

    def row_copy(t, pos):
        return pltpu.make_async_copy(h_ref.at[pl.ds(t, 1), :], xs_ref.at[pl.ds(pos, 1), :], sem)

    def issue(t, carry):
        row_copy(t, pos1_ref[0, 0, t]).start()
        row_copy(t, pos2_ref[0, 0, t]).start()
        return carry
    lax.fori_loop(0, td, issue, 0, unroll=DMA_UNROLL)

    def drain(t, carry):
        row_copy(t, 0).wait()
        row_copy(t, 0).wait()
        return carry
    lax.fori_loop(0, td, drain, 0, unroll=DMA_UNROLL)


def _dispatch(fill, pos1, pos2, rows_in, *, n_rows, td, tm):
    T, W = rows_in.shape
    nblk = T // td
    smem3 = pl.BlockSpec((1, 1, td), lambda i, fl: (i, 0, 0), memory_space=pltpu.SMEM)
    grid_spec = pltpu.PrefetchScalarGridSpec(
        num_scalar_prefetch=1,
        grid=(nblk,),
        in_specs=[smem3, smem3, pl.BlockSpec((td, W), lambda i, fl: (i, 0))],
        out_specs=pl.BlockSpec(memory_space=pl.ANY),
        scratch_shapes=[pltpu.VMEM((tm, W), rows_in.dtype), pltpu.SemaphoreType.DMA(()),
                        pltpu.SemaphoreType.DMA(())],
    )
    return pl.pallas_call(
        functools.partial(_dispatch_body, td=td, tm=tm),
        grid_spec=grid_spec,
        out_shape=jax.ShapeDtypeStruct((n_rows, W), rows_in.dtype),
        compiler_params=_cparams(("arbitrary",)),
        name="dispatch",
    )(fill, pos1.reshape(nblk, 1, td), pos2.reshape(nblk, 1, td), rows_in)


def _expert_weights(plan, j, hbm_refs, buf_refs, bf16_refs, sem):
    texp_ref, first_ref, slot_ref, next_ref = plan

    def copies(e, slot):
        return [pltpu.make_async_copy(h.at[e], b.at[slot], sem.at[i, slot])
                for i, (h, b) in enumerate(zip(hbm_refs, buf_refs))]

    @pl.when(j == 0)
    def _():
        for c in copies(texp_ref[0], 0):
            c.start()

    @pl.when(first_ref[j] == 1)
    def _():
        slot = slot_ref[j]
        for c in copies(texp_ref[j], slot):
            c.wait()

        @pl.when(next_ref[j] >= 0)
        def _():
            for c in copies(next_ref[j], 1 - slot):
                c.start()
        for b, w in zip(buf_refs, bf16_refs):
            w[...] = b[slot].astype(BF16)


def _gateup_body(texp_ref, first_ref, nused_ref, slot_ref, next_ref, x_ref, fg_ref, wg_hbm, wu_hbm,
                 hid_ref, wg_buf, wu_buf, wg_scr, wu_scr, sem):
    j = pl.program_id(0)

    @pl.when(j < nused_ref[0])
    def _():
        _expert_weights((texp_ref, first_ref, slot_ref, next_ref), j, (wg_hbm, wu_hbm),
                        (wg_buf, wu_buf), (wg_scr, wu_scr), sem)
        x = x_ref[...]
        x = (x * lax.rsqrt(jnp.mean(x * x, axis=-1, keepdims=True) + EPS) * fg_ref[...]).astype(BF16)
        g = jnp.dot(x, wg_scr[...], preferred_element_type=F32)
        u = jnp.dot(x, wu_scr[...], preferred_element_type=F32)
        hid_ref[...] = (g * jax.nn.sigmoid(g) * u).astype(BF16)

    @pl.when(j >= nused_ref[0])
    def _():
        hid_ref[...] = jnp.zeros_like(hid_ref)


def _used_tile(j, nused):
    return jnp.minimum(j, nused[0] - 1)


def _gateup(plan, xs, ffn_gain, wg, wu, *, tm):
    P, W = xs.shape
    D, F = wg.shape[1:]
    hbm = pl.BlockSpec(memory_space=pl.ANY)
    grid_spec = pltpu.PrefetchScalarGridSpec(
        num_scalar_prefetch=5,
        grid=(P // tm,),
        in_specs=[pl.BlockSpec((tm, W), lambda j, te, fi, nu, sl, nx: (_used_tile(j, nu), 0)),
                  pl.BlockSpec((1, W), lambda j, te, fi, nu, sl, nx: (0, 0)), hbm, hbm],
        out_specs=pl.BlockSpec((tm, F), lambda j, te, fi, nu, sl, nx: (j, 0)),
        scratch_shapes=[pltpu.VMEM((2, D, F), wg.dtype), pltpu.VMEM((2, D, F), wu.dtype),
                        pltpu.VMEM((D, F), BF16), pltpu.VMEM((D, F), BF16),
                        pltpu.SemaphoreType.DMA((2, 2))],
    )
    return pl.pallas_call(
        _gateup_body,
        grid_spec=grid_spec,
        out_shape=jax.ShapeDtypeStruct((P, F), BF16),
        compiler_params=_cparams(("arbitrary",)),
        name="expert_gateup",
    )(*plan, xs, ffn_gain.reshape(1, W), wg, wu)


def _down_body(texp_ref, first_ref, nused_ref, slot_ref, next_ref, hid_ref, wd_hbm, y_ref,
               wd_buf, wd_scr, sem):
    j = pl.program_id(0)

    @pl.when(j < nused_ref[0])
    def _():
        _expert_weights((texp_ref, first_ref, slot_ref, next_ref), j, (wd_hbm,), (wd_buf,),
                        (wd_scr,), sem)
        y_ref[...] = jnp.dot(hid_ref[...], wd_scr[...], preferred_element_type=F32)

    @pl.when(j >= nused_ref[0])
    def _():
        y_ref[...] = jnp.zeros_like(y_ref)


def _down(plan, hid, wd, *, tm):
    P, F = hid.shape
    D = wd.shape[2]
    grid_spec = pltpu.PrefetchScalarGridSpec(
        num_scalar_prefetch=5,
        grid=(P // tm,),
        in_specs=[pl.BlockSpec((tm, F), lambda j, te, fi, nu, sl, nx: (_used_tile(j, nu), 0)),
                  pl.BlockSpec(memory_space=pl.ANY)],
        out_specs=pl.BlockSpec((tm, D), lambda j, te, fi, nu, sl, nx: (j, 0)),
        scratch_shapes=[pltpu.VMEM((2, F, D), wd.dtype), pltpu.VMEM((F, D), BF16),
                        pltpu.SemaphoreType.DMA((1, 2))],
    )
    return pl.pallas_call(
        _down_body,
        grid_spec=grid_spec,
        out_shape=jax.ShapeDtypeStruct((P, D), F32),
        compiler_params=_cparams(("arbitrary",)),
        name="expert_down",
    )(*plan, hid, wd)


def _combine_body(pos1_ref, pos2_ref, nxt1_ref, nxt2_ref, ys_ref, x1_ref, w1_ref, w2_ref, o_ref,
                  r1_scr, r2_scr, sem, *, tc):
    step = pl.program_id(0)
    slot = step % 2

    def row_copy(pos, dst, s, t):
        return pltpu.make_async_copy(ys_ref.at[pl.ds(pos, 1), :], dst.at[s, pl.ds(t, 1), :], sem.at[s])

    def gather(p1_ref, p2_ref, s):
        def issue(t, carry):
            row_copy(p1_ref[0, 0, t], r1_scr, s, t).start()
            row_copy(p2_ref[0, 0, t], r2_scr, s, t).start()
            return carry
        lax.fori_loop(0, tc, issue, 0, unroll=DMA_UNROLL)

    @pl.when(step == 0)
    def _():
        gather(pos1_ref, pos2_ref, 0)

    @pl.when(step + 1 < pl.num_programs(0))
    def _():
        gather(nxt1_ref, nxt2_ref, 1 - slot)

    def drain(t, carry):
        row_copy(0, r1_scr, slot, t).wait()
        row_copy(0, r2_scr, slot, t).wait()
        return carry
    lax.fori_loop(0, tc, drain, 0, unroll=DMA_UNROLL)

    o_ref[...] = x1_ref[...] + w1_ref[...] * r1_scr[slot] + w2_ref[...] * r2_scr[slot]


def _combine(pos1, pos2, ys, x1, w1, w2, *, tc):
    T, D = x1.shape
    nblk = T // tc
    smem3 = pl.BlockSpec((1, 1, tc), lambda i: (i, 0, 0), memory_space=pltpu.SMEM)
    smem3_next = pl.BlockSpec((1, 1, tc), lambda i: (jnp.minimum(i + 1, nblk - 1), 0, 0),
                              memory_space=pltpu.SMEM)
    col = pl.BlockSpec((tc, 1), lambda i: (i, 0))
    pos1 = pos1.reshape(nblk, 1, tc)
    pos2 = pos2.reshape(nblk, 1, tc)
    return pl.pallas_call(
        functools.partial(_combine_body, tc=tc),
        grid=(nblk,),
        in_specs=[smem3, smem3, smem3_next, smem3_next, pl.BlockSpec(memory_space=pl.ANY),
                  pl.BlockSpec((tc, D), lambda i: (i, 0)), col, col],
        out_specs=pl.BlockSpec((tc, D), lambda i: (i, 0)),
        out_shape=jax.ShapeDtypeStruct((T, D), F32),
        scratch_shapes=[pltpu.VMEM((2, tc, D), F32), pltpu.VMEM((2, tc, D), F32),
                        pltpu.SemaphoreType.DMA((2,))],
        compiler_params=_cparams(("arbitrary",)),
        name="combine",
    )(pos1, pos2, pos1, pos2, ys, x1, w1.reshape(T, 1), w2.reshape(T, 1))


def _tile_sizes(T, D, seq, in_w):
    pick = lambda n, pref: pref if n % pref == 0 else n
    return dict(
        inproj_tm=pick(T, 1024), inproj_tn=next(t for t in (1024, 512, 256, 128) if in_w % t == 0),
        diff_tq=pick(seq, 256),
        outproj_tm=pick(T, 256),
        route_tc=pick(T, 512),
        expert_tm=256,
        dispatch_td=pick(T, 1024), combine_tc=pick(T, 256),
    )


def _layer(x2, p, l, *, batch, seq):
    T, D = x2.shape
    ts = _tile_sizes(T, D, seq, p["w_in"].shape[-1])
    lambda_init = 0.8 - 0.6 * math.exp(-0.3 * l)
    n_groups, n_exp = p["b_router_expert"].shape[1:]
    ne = n_groups * n_exp
    rows = seq // GRID_W

    proj = _inproj(x2, p["mix_norm"][l], p["w_in"][l].astype(BF16),
                   tm=ts["inproj_tm"], tn=ts["inproj_tn"])

    lam = p["diff_lambda"][l].astype(F32)
    lam_full = (jnp.exp(jnp.sum(lam[0] * lam[1])) - jnp.exp(jnp.sum(lam[2] * lam[3]))
                + lambda_init).reshape(1)
    slopes = 2.0 ** (-8.0 * jnp.arange(1, DIFF_HEADS + 1, dtype=F32) / DIFF_HEADS)
    two = lambda g: jnp.tile(g.astype(F32), 2).reshape(1, HEAD_W)
    oa = _diff_attention(proj, _alibi_tiles(slopes, seq), lam_full, two(p["diff_q_norm"][l]), two(p["diff_k_norm"][l]),
                         p["diff_subln"][l].astype(F32).reshape(1, HEAD_W),
                         batch=batch, seq=seq, tq=ts["diff_tq"], out_scale=1.0 - lambda_init)

    ob = _na_attention(proj, p["na_q_norm"][l].astype(F32).reshape(1, NA_DIM),
                       p["na_k_norm"][l].astype(F32).reshape(1, NA_DIM),
                       _na_bias_table(p["na_rpb"][l], rows), batch=batch, seq=seq)

    wr_t = jnp.concatenate([p["w_router_expert"][l].T, p["w_router_group"][l].T,
                            jnp.zeros((ROUTER_ROWS - ne - n_groups, D), F32)], axis=0).astype(F32)
    wr_hi = wr_t.astype(BF16)
    wr_t = jnp.concatenate([wr_hi, (wr_t - wr_hi.astype(F32)).astype(BF16)], axis=0)
    b_col = jnp.concatenate([p["b_router_expert"][l].reshape(ne), p["b_router_group"][l],
                             jnp.zeros((ROUTER_ROWS - ne - n_groups,), F32)]).astype(F32).reshape(ROUTER_ROWS, 1)
    x1, lt = _outproj(oa, ob, proj, x2, p["w_diff_out"][l].astype(BF16),
                          p["w_na_out"][l].astype(BF16), p["w_out"][l].astype(BF16),
                          p["ffn_norm"][l].astype(F32), wr_t, tm=ts["outproj_tm"])

    tm = ts["expert_tm"]
    n_tiles = (T * TOP_K) // tm + ne
    pos, wts, meta = _route(lt, b_col, n_groups=n_groups, n_exp=n_exp, tm=tm, n_tiles=n_tiles,
                            tc=ts["route_tc"])
    fill = meta[2, :n_tiles]
    plan = (meta[0, :n_tiles], meta[1, :n_tiles], meta[3, :1], meta[4, :n_tiles], meta[5, :n_tiles])

    xs = _dispatch(fill, pos[0], pos[1], x1, n_rows=n_tiles * tm, td=ts["dispatch_td"], tm=tm)
    F = p["w_expert_gate"].shape[-1]
    hid = _gateup(plan, xs, p["ffn_norm"][l].astype(F32), p["w_expert_gate"][l].reshape(ne, D, F),
                  p["w_expert_up"][l].reshape(ne, D, F), tm=tm)
    ys = _down(plan, hid, p["w_expert_down"][l].reshape(ne, F, D), tm=tm)
    return _combine(pos[0], pos[1], ys, x1, wts[0], wts[1], tc=ts["combine_tc"])


def kernel(x, mix_norm, w_in, diff_q_norm, diff_k_norm, diff_lambda, diff_subln, na_q_norm, na_k_norm, na_rpb, w_diff_out, w_na_out, w_out, ffn_norm, w_router_group, b_router_group, w_router_expert, b_router_expert, w_expert_gate, w_expert_up, w_expert_down):
    B, S, D = x.shape
    assert S % GRID_W == 0 and S % 128 == 0 and QKV_W % D == 0
    p = dict(mix_norm=mix_norm, w_in=w_in, diff_q_norm=diff_q_norm, diff_k_norm=diff_k_norm,
             diff_lambda=diff_lambda, diff_subln=diff_subln, na_q_norm=na_q_norm,
             na_k_norm=na_k_norm, na_rpb=na_rpb, w_diff_out=w_diff_out, w_na_out=w_na_out,
             w_out=w_out, ffn_norm=ffn_norm, w_router_group=w_router_group,
             b_router_group=b_router_group, w_router_expert=w_router_expert,
             b_router_expert=b_router_expert, w_expert_gate=w_expert_gate,
             w_expert_up=w_expert_up, w_expert_down=w_expert_down)
    x2 = x.reshape(B * S, D).astype(F32)
    for l in range(mix_norm.shape[0]):
        x2 = _layer(x2, p, l, batch=B, seq=S)
    return x2.reshape(B, S, D).astype(x.dtype)
```

```python
import functools
import math

import jax
import jax.numpy as jnp
import numpy as np
from jax import lax
from jax.experimental import pallas as pl
from jax.experimental.pallas import tpu as pltpu

F32 = jnp.float32
BF16 = jnp.bfloat16
I32 = jnp.int32

EPS = 1e-6
LOG2E = math.log2(math.e)
GRID_W = 64
DIFF_HEADS = 8
DIFF_QK_DIM = 64
NA_HEADS = 8
NA_DIM = 128
NA_WIN_ROWS = 8
NA_WIN_COLS = 16
TOP_K = 2
HEAD_W = 128
DQ_BLK, DK_BLK, DV_BLK = 0, DIFF_HEADS, 2 * DIFF_HEADS
NQ_BLK = 3 * DIFF_HEADS
NK_BLK = NQ_BLK + NA_HEADS
NV_BLK = NQ_BLK + 2 * NA_HEADS
QKV_W = (3 * DIFF_HEADS + 3 * NA_HEADS) * HEAD_W

ROUTER_ROWS = 48
VMEM_LIMIT = 56 * 1024 * 1024

_NT = (((1,), (1,)), ((), ()))


def _cparams(sem, vmem=VMEM_LIMIT):
    return pltpu.CompilerParams(dimension_semantics=sem, vmem_limit_bytes=vmem)


def _inproj_body(x_ref, g_ref, w_ref, o_ref, h_scr, *, rows):
    @pl.when(pl.program_id(1) == 0)
    def _():
        def chunk(c, carry):
            r = pl.ds(pl.multiple_of(c * rows, rows), rows)
            x = x_ref[r, :]
            ms = jnp.mean(x * x, axis=-1, keepdims=True)
            h_scr[r, :] = (x * lax.rsqrt(ms + EPS) * g_ref[...]).astype(BF16)
            return carry
        lax.fori_loop(0, x_ref.shape[0] // rows, chunk, 0)

    o_ref[...] = jnp.dot(h_scr[...], w_ref[...], preferred_element_type=F32).astype(BF16)


def _inproj(x2, gain, w_bf16, *, tm, tn):
    T, D = x2.shape
    N = w_bf16.shape[1]
    return pl.pallas_call(
        functools.partial(_inproj_body, rows=128),
        grid=(T // tm, N // tn),
        in_specs=[pl.BlockSpec((tm, D), lambda i, j: (i, 0)),
                  pl.BlockSpec((1, D), lambda i, j: (0, 0)),
                  pl.BlockSpec((D, tn), lambda i, j: (0, j))],
        out_specs=pl.BlockSpec((tm, tn), lambda i, j: (i, j)),
        out_shape=jax.ShapeDtypeStruct((T, N), BF16),
        scratch_shapes=[pltpu.VMEM((tm, D), BF16)],
        compiler_params=_cparams(("parallel", "arbitrary")),
        name="inproj",
    )(x2, gain.reshape(1, D), w_bf16)


def _half_rms(x, gain, first):
    x2 = x * x
    s0 = jnp.sum(jnp.where(first, x2, 0.0), axis=-1, keepdims=True)
    s1 = jnp.sum(jnp.where(first, 0.0, x2), axis=-1, keepdims=True)
    inv = jnp.where(first, lax.rsqrt(s0 * (1.0 / DIFF_QK_DIM) + EPS),
                    lax.rsqrt(s1 * (1.0 / DIFF_QK_DIM) + EPS))
    return x * inv * gain


def _alibi_tiles(slopes, seq):
    nb = seq // HEAD_W
    off = np.arange(HEAD_W)
    delta = (np.arange(-(nb - 1), nb)[:, None, None] * HEAD_W + off[None, None, :] - off[None, :, None])
    dist = jnp.asarray(np.abs(delta).astype(np.float32))
    return (-LOG2E * slopes.astype(F32))[:, None, None, None] * dist


def _diff_body(lam_ref, q_ref, k_ref, v_ref, qg_ref, kg_ref, sg_ref, bias_ref, o_ref,
               qm_scr, kn_scr, vt_scr, s0_scr, s1_scr, p_scr, ot_scr, *, tq, seq, out_scale):
    nkb = seq // HEAD_W
    nq = seq // tq
    qblks = tq // HEAD_W
    groups = HEAD_W // 8
    lane = lax.broadcasted_iota(I32, (1, HEAD_W), 1)
    first = lane < DIFF_QK_DIM
    kn_scr[...] = _half_rms(k_ref[...].astype(F32), kg_ref[...], first).astype(BF16)
    vt_scr[:HEAD_W, :] = v_ref[...].astype(F32).T.astype(BF16)
    vt_scr[HEAD_W:, :] = jnp.ones((8, seq), BF16)
    q = _half_rms(q_ref[...].astype(F32), qg_ref[...], first) * (DIFF_QK_DIM ** -0.5 * LOG2E)
    qm_scr[0] = jnp.where(first, q, 0.0).astype(BF16)
    qm_scr[1] = jnp.where(first, 0.0, q).astype(BF16)
    lam = lam_ref[0]

    def stage(score, expo):
        mx = [jnp.full((8, tq), -jnp.inf, F32), jnp.full((8, tq), -jnp.inf, F32)]
        for c in range(nkb):
            ks = slice(c * HEAD_W, (c + 1) * HEAD_W)
            if score is not None:
                qa, dst = score
                rows = pl.ds(pl.multiple_of(qa * tq, tq), tq)
                kc = kn_scr[ks, :]
                tile0 = qa * qblks - c + (nkb - 1)
                bias = jnp.concatenate([bias_ref[0, tile0 + t] for t in range(qblks)], axis=1)
                for m in range(2):
                    s = lax.dot_general(kc, qm_scr[m, rows, :], _NT,
                                        preferred_element_type=F32) + bias
                    dst[m, ks, :] = s
                    mx[m] = jnp.maximum(mx[m], jnp.max(s.reshape(groups, 8, tq), axis=0))
            if expo is not None:
                src, mrow = expo
                for m in range(2):
                    p_scr[m, ks, :] = jnp.exp2((src[m, ks, :] - mrow[m]).astype(BF16))
        return tuple(jnp.max(a, axis=0, keepdims=True) for a in mx)

    def values(qb):
        outs = []
        for m in range(2):
            ol = jnp.dot(vt_scr[...], p_scr[m], preferred_element_type=F32)
            outs.append(ol[:HEAD_W] / ol[HEAD_W:HEAD_W + 1])
        ot_scr[qb] = outs[0] - lam * outs[1]

    mrow = stage((0, s0_scr), None)

    def pair(j, mrow):
        mrow1 = stage((2 * j + 1, s1_scr), (s0_scr, mrow))
        values(2 * j)
        mrow0 = stage((2 * j + 2, s0_scr), (s1_scr, mrow1))
        values(2 * j + 1)
        return mrow0
    mrow = lax.fori_loop(0, nq // 2 - 1, pair, mrow)
    mrow1 = stage((nq - 1, s1_scr), (s0_scr, mrow))
    values(nq - 2)
    stage(None, (s1_scr, mrow1))
    values(nq - 1)

    gain = sg_ref[...] * out_scale
    for i in range(nq):
        o = ot_scr[i]
        o = o * lax.rsqrt(jnp.mean(o * o, axis=0, keepdims=True) + EPS)
        o_ref[i * tq:(i + 1) * tq, :] = (o.T * gain).astype(BF16)


def _diff_attention(proj, bias_tiles, lam, q_gain, k_gain, sub_gain, *, batch, seq, tq, out_scale):
    T = proj.shape[0]
    smem = pl.BlockSpec(memory_space=pltpu.SMEM)
    vec = pl.BlockSpec((1, HEAD_W), lambda h, b: (0, 0))
    blk = lambda col: pl.BlockSpec((seq, HEAD_W), lambda h, b: (b, col + h))
    return pl.pallas_call(
        functools.partial(_diff_body, tq=tq, seq=seq, out_scale=out_scale),
        grid=(DIFF_HEADS, batch),
        in_specs=[smem, blk(DQ_BLK), blk(DK_BLK), blk(DV_BLK), vec, vec, vec,
                  pl.BlockSpec((1,) + bias_tiles.shape[1:], lambda h, b: (h, 0, 0, 0))],
        out_specs=pl.BlockSpec((seq, HEAD_W), lambda h, b: (b, h)),
        out_shape=jax.ShapeDtypeStruct((T, DIFF_HEADS * HEAD_W), BF16),
        scratch_shapes=[pltpu.VMEM((2, seq, HEAD_W), BF16), pltpu.VMEM((seq, HEAD_W), BF16),
                        pltpu.VMEM((HEAD_W + 8, seq), BF16),
                        pltpu.VMEM((2, seq, tq), F32), pltpu.VMEM((2, seq, tq), F32),
                        pltpu.VMEM((2, seq, tq), BF16), pltpu.VMEM((seq // tq, HEAD_W, tq), F32)],
        compiler_params=_cparams(("parallel", "parallel")),
        name="diff_attn",
    )(lam, proj, proj, proj, q_gain, k_gain, sub_gain, bias_tiles)


NA_GROUP_ROWS = 4


def _na_plan(rows):
    wr = min(NA_WIN_ROWS, rows)
    gs = NA_GROUP_ROWS if rows % NA_GROUP_ROWS == 0 else 1
    ww = wr + gs - 1
    ww = min(ww + ww % 2, rows)
    row_start = lambda r: min(max(r - wr // 2, 0), rows - wr)
    groups, patterns = [], {}
    for g in range(rows // gs):
        ws = min(row_start(g * gs), rows - ww)
        key = tuple((row_start(r) - ws, r - ws) for r in range(g * gs, (g + 1) * gs))
        groups.append((ws, patterns.setdefault(key, len(patterns))))
    return wr, gs, ww, groups, list(patterns)


def _na_bias_table(rpb, rows):
    wr, gs, ww, _, patterns = _na_plan(rows)
    H = rpb.shape[0]
    c = np.arange(GRID_W)
    col_start = np.clip(c - NA_WIN_COLS // 2, 0, GRID_W - NA_WIN_COLS)
    col_in = (c[None, :] >= col_start[:, None]) & (c[None, :] < col_start[:, None] + NA_WIN_COLS)
    dc = np.clip(c[None, :] - c[:, None], -(NA_WIN_COLS - 1), NA_WIN_COLS - 1) + (NA_WIN_COLS - 1)
    sel = jnp.asarray((dc[:, :, None] == np.arange(2 * NA_WIN_COLS - 1)[None, None, :]).astype(np.float32))
    scaled = rpb.astype(F32) * LOG2E
    pats = []
    for key in patterns:
        blocks = []
        for win_off, row_off in key:
            lo = win_off - row_off + NA_WIN_ROWS - 1
            vals = jnp.einsum("hwd,ckd->hcwk", scaled[:, lo:lo + wr, :], sel,
                              precision=lax.Precision.HIGHEST)
            vals = jnp.where(jnp.asarray(col_in)[None, :, None, :], vals, -jnp.inf)
            pad = lambda n: jnp.full((H, GRID_W, n, GRID_W), -jnp.inf, F32)
            full = jnp.concatenate([pad(win_off), vals, pad(ww - win_off - wr)], axis=2)
            blocks.append(full.reshape(H, GRID_W, ww * GRID_W))
        pats.append(jnp.concatenate(blocks, axis=1))
    return jnp.stack(pats, axis=1)


def _na_body(q_ref, k_ref, v_ref, qg_ref, kg_ref, bias_ref, o_ref, qn_scr, kn_scr, va_scr, *, rows):
    wr, gs, ww, groups, _ = _na_plan(rows)

    def rms(x, g):
        return x * lax.rsqrt(jnp.mean(x * x, axis=-1, keepdims=True) + EPS) * g

    qn_scr[...] = (rms(q_ref[...].astype(F32), qg_ref[...]) * (NA_DIM ** -0.5 * LOG2E)).astype(BF16)
    kn_scr[...] = rms(k_ref[...].astype(F32), kg_ref[...]).astype(BF16)
    va_scr[:, :NA_DIM] = v_ref[...]
    va_scr[:, NA_DIM:] = jnp.ones((v_ref.shape[0], NA_DIM), BF16)

    for g, (ws, pid) in enumerate(groups):
        qs = slice(g * gs * GRID_W, (g + 1) * gs * GRID_W)
        ks = slice(ws * GRID_W, (ws + ww) * GRID_W)
        s = lax.dot_general(qn_scr[qs, :], kn_scr[ks, :], _NT, preferred_element_type=F32)
        s = s + bias_ref[0, pid]
        m = jnp.max(s, axis=-1, keepdims=True)
        p = jnp.exp2((s - m).astype(BF16))
        ol = jnp.dot(p, va_scr[ks, :], preferred_element_type=F32)
        o_ref[qs, :] = (ol[:, :NA_DIM] / ol[:, NA_DIM:]).astype(BF16)


def _na_attention(proj, q_gain, k_gain, bias_tab, *, batch, seq):
    T = proj.shape[0]
    rows = seq // GRID_W
    vec = pl.BlockSpec((1, NA_DIM), lambda h, b: (0, 0))
    return pl.pallas_call(
        functools.partial(_na_body, rows=rows),
        grid=(NA_HEADS, batch),
        in_specs=[pl.BlockSpec((seq, HEAD_W), lambda h, b: (b, NQ_BLK + h)),
                  pl.BlockSpec((seq, HEAD_W), lambda h, b: (b, NK_BLK + h)),
                  pl.BlockSpec((seq, HEAD_W), lambda h, b: (b, NV_BLK + h)),
                  vec, vec,
                  pl.BlockSpec((1,) + bias_tab.shape[1:], lambda h, b: (h, 0, 0, 0))],
        out_specs=pl.BlockSpec((seq, HEAD_W), lambda h, b: (b, h)),
        out_shape=jax.ShapeDtypeStruct((T, NA_HEADS * NA_DIM), BF16),
        scratch_shapes=[pltpu.VMEM((seq, HEAD_W), BF16), pltpu.VMEM((seq, HEAD_W), BF16),
                        pltpu.VMEM((seq, 2 * NA_DIM), BF16)],
        compiler_params=_cparams(("parallel", "parallel")),
        name="na_attn",
    )(proj, proj, proj, q_gain, k_gain, bias_tab)


def _outproj_body(oa_ref, ob_ref, ga_ref, gb_ref, x_ref, wd_ref, wn_ref, wo_ref, fg_ref, wr_ref,
                  x1_ref, lt_ref, m_scr, *, sub, ncol):
    tm, D = x_ref.shape
    for r in range(tm // sub):
        rs = slice(r * sub, (r + 1) * sub)
        oa = oa_ref[rs, :]
        ob = ob_ref[rs, :]
        for c in range(D // ncol):
            cs = slice(c * ncol, (c + 1) * ncol)
            ya = jnp.dot(oa, wd_ref[:, cs], preferred_element_type=F32)
            yb = jnp.dot(ob, wn_ref[:, cs], preferred_element_type=F32)
            m_scr[rs, cs] = (jax.nn.sigmoid(ga_ref[rs, cs].astype(F32)) * ya
                             + jax.nn.sigmoid(gb_ref[rs, cs].astype(F32)) * yb).astype(BF16)
        x1 = x_ref[rs, :] + jnp.dot(m_scr[rs, :], wo_ref[...], preferred_element_type=F32)
        x1_ref[rs, :] = x1
        h2 = x1 * lax.rsqrt(jnp.mean(x1 * x1, axis=-1, keepdims=True) + EPS) * fg_ref[...]
        h_hi = h2.astype(BF16)
        h_lo = (h2 - h_hi.astype(F32)).astype(BF16)
        both = lax.dot_general(wr_ref[...], h_hi, _NT, preferred_element_type=F32)
        cross = lax.dot_general(wr_ref[:ROUTER_ROWS, :], h_lo, _NT, preferred_element_type=F32)
        lt_ref[:, rs] = both[:ROUTER_ROWS] + both[ROUTER_ROWS:] + cross


def _outproj(oa, ob, proj, x2, wd, wn, wo, ffn_gain, wr_t, *, tm):
    T, D = x2.shape
    gate_blk = QKV_W // D
    const = lambda shape: pl.BlockSpec(shape, lambda i: (0, 0), pipeline_mode=pl.Buffered(1))
    return pl.pallas_call(
        functools.partial(_outproj_body, sub=min(tm, 256), ncol=min(D, 1024)),
        grid=(T // tm,),
        in_specs=[pl.BlockSpec((tm, oa.shape[1]), lambda i: (i, 0)),
                  pl.BlockSpec((tm, ob.shape[1]), lambda i: (i, 0)),
                  pl.BlockSpec((tm, D), lambda i: (i, gate_blk)),
                  pl.BlockSpec((tm, D), lambda i: (i, gate_blk + 1)),
                  pl.BlockSpec((tm, D), lambda i: (i, 0)),
                  const(wd.shape), const(wn.shape), const(wo.shape),
                  const((1, D)), const(wr_t.shape)],
        out_specs=[pl.BlockSpec((tm, D), lambda i: (i, 0)),
                   pl.BlockSpec((ROUTER_ROWS, tm), lambda i: (0, i))],
        out_shape=[jax.ShapeDtypeStruct((T, D), F32),
                   jax.ShapeDtypeStruct((ROUTER_ROWS, T), F32)],
        scratch_shapes=[pltpu.VMEM((tm, D), BF16)],
        compiler_params=_cparams(("parallel",)),
        name="outproj",
    )(oa, ob, proj, proj, x2, wd, wn, wo, ffn_gain.reshape(1, D), wr_t)


def _route_body(lt_ref, bias_ref, pos_ref, wts_ref, meta_ref, cnt_scr, carry_scr,
                *, n_groups, n_exp, tm, tc):
    ne = n_groups * n_exp
    phase = pl.program_id(0)
    step = pl.program_id(1)

    l = lt_ref[...] + bias_ref[...]
    el = l[0:ne]
    gl = l[ne:ne + n_groups]
    gmax = jnp.max(gl, axis=0, keepdims=True)
    gi = lax.broadcasted_iota(I32, gl.shape, 0)
    gsel = jnp.min(jnp.where(gl == gmax, gi, n_groups), axis=0, keepdims=True)
    gw = 1.0 / jnp.sum(jnp.exp(gl - gmax), axis=0, keepdims=True)
    ei = lax.broadcasted_iota(I32, el.shape, 0)
    lo = gsel * n_exp
    elm = jnp.where((ei >= lo) & (ei < lo + n_exp), el, -jnp.inf)
    m1 = jnp.max(elm, axis=0, keepdims=True)
    i1 = jnp.min(jnp.where(elm == m1, ei, ne), axis=0, keepdims=True)
    elm2 = jnp.where(ei == i1, -jnp.inf, elm)
    m2 = jnp.max(elm2, axis=0, keepdims=True)
    i2 = jnp.min(jnp.where(elm2 == m2, ei, ne), axis=0, keepdims=True)
    r = jnp.exp(m2 - m1)
    w1 = gw / (1.0 + r)
    w2 = w1 * r
    oh1 = ei == i1
    oh2 = ei == i2
    both = jnp.where(oh1 | oh2, 1.0, 0.0)

    @pl.when((phase == 0) & (step == 0))
    def _():
        cnt_scr[...] = jnp.zeros_like(cnt_scr)
        carry_scr[...] = jnp.zeros_like(carry_scr)

    @pl.when(phase == 0)
    def _():
        cnt_scr[...] += jnp.sum(both, axis=1, keepdims=True)

    @pl.when(phase == 1)
    def _():
        cnt = cnt_scr[...]
        tiles = jnp.floor((cnt + (tm - 1)) * (1.0 / tm))
        ea = lax.broadcasted_iota(I32, (ne, ne), 0)
        eb = lax.broadcasted_iota(I32, (ne, ne), 1)
        lower = jnp.where(eb < ea, 1.0, 0.0).astype(BF16)
        tiles_b = jnp.broadcast_to(tiles, (ne, 128)).astype(BF16)
        start_t = jnp.dot(lower, tiles_b, preferred_element_type=F32)[:, 0:1]
        ta = lax.broadcasted_iota(I32, (tc, tc), 0)
        tb = lax.broadcasted_iota(I32, (tc, tc), 1)
        upper = jnp.where(ta < tb, 1.0, 0.0).astype(BF16)
        prefix = jnp.dot(both.astype(BF16), upper, preferred_element_type=F32)
        base = prefix + carry_scr[...] + start_t * tm
        pos1 = jnp.sum(jnp.where(oh1, base, 0.0), axis=0, keepdims=True)
        pos2 = jnp.sum(jnp.where(oh2, base, 0.0), axis=0, keepdims=True)
        carry_scr[...] += jnp.sum(both, axis=1, keepdims=True)
        pos_ref[...] = jnp.zeros_like(pos_ref)
        pos_ref[0:1, :] = pos1.astype(I32)
        pos_ref[1:2, :] = pos2.astype(I32)
        wts_ref[...] = jnp.zeros_like(wts_ref)
        wts_ref[0:1, :] = w1
        wts_ref[1:2, :] = w2

        nt = meta_ref.shape[1]
        tj = lax.broadcasted_iota(I32, (ne, nt), 1).astype(F32)
        end_t = start_t + tiles
        texp = jnp.sum(jnp.where(end_t <= tj, 1.0, 0.0), axis=0, keepdims=True)
        is_first = jnp.sum(jnp.where((start_t == tj) & (tiles > 0.0), 1.0, 0.0),
                           axis=0, keepdims=True)
        is_last = jnp.sum(jnp.where((end_t - 1.0 == tj) & (tiles > 0.0), 1.0, 0.0),
                          axis=0, keepdims=True)
        nused = jnp.broadcast_to(jnp.sum(tiles, axis=0, keepdims=True), (1, nt))
        meta_ref[...] = jnp.zeros_like(meta_ref)
        meta_ref[0:1, :] = jnp.minimum(texp, ne - 1.0).astype(I32)
        meta_ref[1:2, :] = is_first.astype(I32)
        fill = is_last + jnp.where(tj[0:1, :] >= nused, 1.0, 0.0)
        meta_ref[2:3, :] = fill.astype(I32)
        meta_ref[3:4, :] = nused.astype(I32)
        used = tiles > 0.0
        seg = jnp.sum(jnp.where((end_t <= tj) & used, 1.0, 0.0), axis=0, keepdims=True)
        eidx = lax.broadcasted_iota(I32, (ne, nt), 0).astype(F32)
        nxt = jnp.min(jnp.where((start_t > tj) & used, eidx, float(ne)), axis=0, keepdims=True)
        meta_ref[4:5, :] = (seg - 2.0 * jnp.floor(seg * 0.5)).astype(I32)
        meta_ref[5:6, :] = jnp.where(nxt >= ne, -1.0, nxt).astype(I32)


def _route(lt, bias_col, *, n_groups, n_exp, tm, n_tiles, tc):
    T = lt.shape[1]
    ne = n_groups * n_exp
    nt_pad = -(-n_tiles // 128) * 128
    return pl.pallas_call(
        functools.partial(_route_body, n_groups=n_groups, n_exp=n_exp, tm=tm, tc=tc),
        grid=(2, T // tc),
        in_specs=[pl.BlockSpec((ROUTER_ROWS, tc), lambda p, i: (0, i)),
                  pl.BlockSpec((ROUTER_ROWS, 1), lambda p, i: (0, 0))],
        out_specs=[pl.BlockSpec((8, tc), lambda p, i: (0, i * p)),
                   pl.BlockSpec((8, tc), lambda p, i: (0, i * p)),
                   pl.BlockSpec((8, nt_pad), lambda p, i: (0, 0))],
        out_shape=[jax.ShapeDtypeStruct((8, T), I32),
                   jax.ShapeDtypeStruct((8, T), F32),
                   jax.ShapeDtypeStruct((8, nt_pad), I32)],
        scratch_shapes=[pltpu.VMEM((ne, 1), F32), pltpu.VMEM((ne, 1), F32)],
        compiler_params=_cparams(("arbitrary", "arbitrary")),
        name="route",
    )(lt, bias_col)


DMA_UNROLL = 8


def _dispatch_body(fill_ref, pos1_ref, pos2_ref, h_ref, xs_ref, zero_scr, sem, zsem, *, td, tm):
    n_tiles = fill_ref.shape[0]

    def zero_copy(j):
        return pltpu.make_async_copy(zero_scr, xs_ref.at[pl.ds(pl.multiple_of(j * tm, tm), tm), :], zsem)

    @pl.when(pl.program_id(0) == 0)
    def _():
        zero_scr[...] = jnp.zeros_like(zero_scr)

        def start(j, carry):
            @pl.when(fill_ref[j] == 1)
            def _():
                zero_copy(j).start()
            return carry
        lax.fori_loop(0, n_tiles, start, 0)

        def wait(j, carry):
            @pl.when(fill_ref[j] == 1)
            def _():
                zero_copy(j).wait()
            return carry
        lax.fori_loop(0, n_tiles, wait, 0)

    def row_copy(t, pos):
        return pltpu.make_async_copy(h_ref.at[pl.ds(t, 1), :], xs_ref.at[pl.ds(pos, 1), :], sem)

    def issue(t, carry):
        row_copy(t, pos1_ref[0, 0, t]).start()
        row_copy(t, pos2_ref[0, 0, t]).start()
        return carry
    lax.fori_loop(0, td, issue, 0, unroll=DMA_UNROLL)

    def drain(t, carry):
        row_copy(t, 0).wait()
        row_copy(t, 0).wait()
        return carry
    lax.fori_loop(0, td, drain, 0, unroll=DMA_UNROLL)


def _dispatch(fill, pos1, pos2, rows_in, *, n_rows, td, tm):
    T, W = rows_in.shape
    nblk = T // td
    smem3 = pl.BlockSpec((1, 1, td), lambda i, fl: (i, 0, 0), memory_space=pltpu.SMEM)
    grid_spec = pltpu.PrefetchScalarGridSpec(
        num_scalar_prefetch=1,
        grid=(nblk,),
        in_specs=[smem3, smem3, pl.BlockSpec((td, W), lambda i, fl: (i, 0))],
        out_specs=pl.BlockSpec(memory_space=pl.ANY),
        scratch_shapes=[pltpu.VMEM((tm, W), rows_in.dtype), pltpu.SemaphoreType.DMA(()),
                        pltpu.SemaphoreType.DMA(())],
    )
    return pl.pallas_call(
        functools.partial(_dispatch_body, td=td, tm=tm),
        grid_spec=grid_spec,
        out_shape=jax.ShapeDtypeStruct((n_rows, W), rows_in.dtype),
        compiler_params=_cparams(("arbitrary",)),
        name="dispatch",
    )(fill, pos1.reshape(nblk, 1, td), pos2.reshape(nblk, 1, td), rows_in)


def _expert_weights(plan, j, hbm_refs, buf_refs, bf16_refs, sem):
    texp_ref, first_ref, slot_ref, next_ref = plan

    def copies(e, slot):
        return [pltpu.make_async_copy(h.at[e], b.at[slot], sem.at[i, slot])
                for i, (h, b) in enumerate(zip(hbm_refs, buf_refs))]

    @pl.when(j == 0)
    def _():
        for c in copies(texp_ref[0], 0):
            c.start()

    @pl.when(first_ref[j] == 1)
    def _():
        slot = slot_ref[j]
        for c in copies(texp_ref[j], slot):
            c.wait()

        @pl.when(next_ref[j] >= 0)
        def _():
            for c in copies(next_ref[j], 1 - slot):
                c.start()
        for b, w in zip(buf_refs, bf16_refs):
            w[...] = b[slot].astype(BF16)


def _gateup_body(texp_ref, first_ref, nused_ref, slot_ref, next_ref, x_ref, fg_ref, wg_hbm, wu_hbm,
                 hid_ref, wg_buf, wu_buf, wg_scr, wu_scr, sem):
    j = pl.program_id(0)

    @pl.when(j < nused_ref[0])
    def _():
        _expert_weights((texp_ref, first_ref, slot_ref, next_ref), j, (wg_hbm, wu_hbm),
                        (wg_buf, wu_buf), (wg_scr, wu_scr), sem)
        x = x_ref[...]
        x = (x * lax.rsqrt(jnp.mean(x * x, axis=-1, keepdims=True) + EPS) * fg_ref[...]).astype(BF16)
        g = jnp.dot(x, wg_scr[...], preferred_element_type=F32)
        u = jnp.dot(x, wu_scr[...], preferred_element_type=F32)
        hid_ref[...] = (g * jax.nn.sigmoid(g) * u).astype(BF16)

    @pl.when(j >= nused_ref[0])
    def _():
        hid_ref[...] = jnp.zeros_like(hid_ref)


def _used_tile(j, nused):
    return jnp.minimum(j, nused[0] - 1)


def _gateup(plan, xs, ffn_gain, wg, wu, *, tm):
    P, W = xs.shape
    D, F = wg.shape[1:]
    hbm = pl.BlockSpec(memory_space=pl.ANY)
    grid_spec = pltpu.PrefetchScalarGridSpec(
        num_scalar_prefetch=5,
        grid=(P // tm,),
        in_specs=[pl.BlockSpec((tm, W), lambda j, te, fi, nu, sl, nx: (_used_tile(j, nu), 0)),
                  pl.BlockSpec((1, W), lambda j, te, fi, nu, sl, nx: (0, 0)), hbm, hbm],
        out_specs=pl.BlockSpec((tm, F), lambda j, te, fi, nu, sl, nx: (j, 0)),
        scratch_shapes=[pltpu.VMEM((2, D, F), wg.dtype), pltpu.VMEM((2, D, F), wu.dtype),
                        pltpu.VMEM((D, F), BF16), pltpu.VMEM((D, F), BF16),
                        pltpu.SemaphoreType.DMA((2, 2))],
    )
    return pl.pallas_call(
        _gateup_body,
        grid_spec=grid_spec,
        out_shape=jax.ShapeDtypeStruct((P, F), BF16),
        compiler_params=_cparams(("arbitrary",)),
        name="expert_gateup",
    )(*plan, xs, ffn_gain.reshape(1, W), wg, wu)


def _down_body(texp_ref, first_ref, nused_ref, slot_ref, next_ref, hid_ref, wd_hbm, y_ref,
               wd_buf, wd_scr, sem):
    j = pl.program_id(0)

    @pl.when(j < nused_ref[0])
    def _():
        _expert_weights((texp_ref, first_ref, slot_ref, next_ref), j, (wd_hbm,), (wd_buf,),
                        (wd_scr,), sem)
        y_ref[...] = jnp.dot(hid_ref[...], wd_scr[...], preferred_element_type=F32)

    @pl.when(j >= nused_ref[0])
    def _():
        y_ref[...] = jnp.zeros_like(y_ref)


def _down(plan, hid, wd, *, tm):
    P, F = hid.shape
    D = wd.shape[2]
    grid_spec = pltpu.PrefetchScalarGridSpec(
        num_scalar_prefetch=5,
        grid=(P // tm,),
        in_specs=[pl.BlockSpec((tm, F), lambda j, te, fi, nu, sl, nx: (_used_tile(j, nu), 0)),
                  pl.BlockSpec(memory_space=pl.ANY)],
        out_specs=pl.BlockSpec((tm, D), lambda j, te, fi, nu, sl, nx: (j, 0)),
        scratch_shapes=[pltpu.VMEM((2, F, D), wd.dtype), pltpu.VMEM((F, D), BF16),
                        pltpu.SemaphoreType.DMA((1, 2))],
    )
    return pl.pallas_call(
        _down_body,
        grid_spec=grid_spec,
        out_shape=jax.ShapeDtypeStruct((P, D), F32),
        compiler_params=_cparams(("arbitrary",)),
        name="expert_down",
    )(*plan, hid, wd)


def _combine_body(pos1_ref, pos2_ref, nxt1_ref, nxt2_ref, ys_ref, x1_ref, w1_ref, w2_ref, o_ref,
                  r1_scr, r2_scr, sem, *, tc):
    step = pl.program_id(0)
    slot = step % 2

    def row_copy(pos, dst, s, t):
        return pltpu.make_async_copy(ys_ref.at[pl.ds(pos, 1), :], dst.at[s, pl.ds(t, 1), :], sem.at[s])

    def gather(p1_ref, p2_ref, s):
        def issue(t, carry):
            row_copy(p1_ref[0, 0, t], r1_scr, s, t).start()
            row_copy(p2_ref[0, 0, t], r2_scr, s, t).start()
            return carry
        lax.fori_loop(0, tc, issue, 0, unroll=DMA_UNROLL)

    @pl.when(step == 0)
    def _():
        gather(pos1_ref, pos2_ref, 0)

    @pl.when(step + 1 < pl.num_programs(0))
    def _():
        gather(nxt1_ref, nxt2_ref, 1 - slot)

    def drain(t, carry):
        row_copy(0, r1_scr, slot, t).wait()
        row_copy(0, r2_scr, slot, t).wait()
        return carry
    lax.fori_loop(0, tc, drain, 0, unroll=DMA_UNROLL)

    o_ref[...] = x1_ref[...] + w1_ref[...] * r1_scr[slot] + w2_ref[...] * r2_scr[slot]


def _combine(pos1, pos2, ys, x1, w1, w2, *, tc):
    T, D = x1.shape
    nblk = T // tc
    smem3 = pl.BlockSpec((1, 1, tc), lambda i: (i, 0, 0), memory_space=pltpu.SMEM)
    smem3_next = pl.BlockSpec((1, 1, tc), lambda i: (jnp.minimum(i + 1, nblk - 1), 0, 0),
                              memory_space=pltpu.SMEM)
    col = pl.BlockSpec((tc, 1), lambda i: (i, 0))
    pos1 = pos1.reshape(nblk, 1, tc)
    pos2 = pos2.reshape(nblk, 1, tc)
    return pl.pallas_call(
        functools.partial(_combine_body, tc=tc),
        grid=(nblk,),
        in_specs=[smem3, smem3, smem3_next, smem3_next, pl.BlockSpec(memory_space=pl.ANY),
                  pl.BlockSpec((tc, D), lambda i: (i, 0)), col, col],
        out_specs=pl.BlockSpec((tc, D), lambda i: (i, 0)),
        out_shape=jax.ShapeDtypeStruct((T, D), F32),
        scratch_shapes=[pltpu.VMEM((2, tc, D), F32), pltpu.VMEM((2, tc, D), F32),
                        pltpu.SemaphoreType.DMA((2,))],
        compiler_params=_cparams(("arbitrary",)),
        name="combine",
    )(pos1, pos2, pos1, pos2, ys, x1, w1.reshape(T, 1), w2.reshape(T, 1))


def _tile_sizes(T, D, seq, in_w):
    pick = lambda n, pref: pref if n % pref == 0 else n
    return dict(
        inproj_tm=pick(T, 1024), inproj_tn=next(t for t in (2048, 1024, 512, 256, 128) if in_w % t == 0),
        diff_tq=pick(seq, 256),
        outproj_tm=pick(T, 512),
        route_tc=pick(T, 1024),
        expert_tm=256,
        dispatch_td=pick(T, 1024), combine_tc=pick(T, 256),
    )


def _layer(x2, p, l, *, batch, seq):
    T, D = x2.shape
    ts = _tile_sizes(T, D, seq, p["w_in"].shape[-1])
    lambda_init = 0.8 - 0.6 * math.exp(-0.3 * l)
    n_groups, n_exp = p["b_router_expert"].shape[1:]
    ne = n_groups * n_exp
    rows = seq // GRID_W

    proj = _inproj(x2, p["mix_norm"][l], p["w_in"][l].astype(BF16),
                   tm=ts["inproj_tm"], tn=ts["inproj_tn"])

    lam = p["diff_lambda"][l].astype(F32)
    lam_full = (jnp.exp(jnp.sum(lam[0] * lam[1])) - jnp.exp(jnp.sum(lam[2] * lam[3]))
                + lambda_init).reshape(1)
    slopes = 2.0 ** (-8.0 * jnp.arange(1, DIFF_HEADS + 1, dtype=F32) / DIFF_HEADS)
    two = lambda g: jnp.tile(g.astype(F32), 2).reshape(1, HEAD_W)
    oa = _diff_attention(proj, _alibi_tiles(slopes, seq), lam_full, two(p["diff_q_norm"][l]), two(p["diff_k_norm"][l]),
                         p["diff_subln"][l].astype(F32).reshape(1, HEAD_W),
                         batch=batch, seq=seq, tq=ts["diff_tq"], out_scale=1.0 - lambda_init)

    ob = _na_attention(proj, p["na_q_norm"][l].astype(F32).reshape(1, NA_DIM),
                       p["na_k_norm"][l].astype(F32).reshape(1, NA_DIM),
                       _na_bias_table(p["na_rpb"][l], rows), batch=batch, seq=seq)

    wr_t = jnp.concatenate([p["w_router_expert"][l].T, p["w_router_group"][l].T,
                            jnp.zeros((ROUTER_ROWS - ne - n_groups, D), F32)], axis=0).astype(F32)
    wr_hi = wr_t.astype(BF16)
    wr_t = jnp.concatenate([wr_hi, (wr_t - wr_hi.astype(F32)).astype(BF16)], axis=0)
    b_col = jnp.concatenate([p["b_router_expert"][l].reshape(ne), p["b_router_group"][l],
                             jnp.zeros((ROUTER_ROWS - ne - n_groups,), F32)]).astype(F32).reshape(ROUTER_ROWS, 1)
    x1, lt = _outproj(oa, ob, proj, x2, p["w_diff_out"][l].astype(BF16),
                          p["w_na_out"][l].astype(BF16), p["w_out"][l].astype(BF16),
                          p["ffn_norm"][l].astype(F32), wr_t, tm=ts["outproj_tm"])

    tm = ts["expert_tm"]
    n_tiles = (T * TOP_K) // tm + ne
    pos, wts, meta = _route(lt, b_col, n_groups=n_groups, n_exp=n_exp, tm=tm, n_tiles=n_tiles,
                            tc=ts["route_tc"])
    fill = meta[2, :n_tiles]
    plan = (meta[0, :n_tiles], meta[1, :n_tiles], meta[3, :1], meta[4, :n_tiles], meta[5, :n_tiles])

    xs = _dispatch(fill, pos[0], pos[1], x1, n_rows=n_tiles * tm, td=ts["dispatch_td"], tm=tm)
    F = p["w_expert_gate"].shape[-1]
    hid = _gateup(plan, xs, p["ffn_norm"][l].astype(F32), p["w_expert_gate"][l].reshape(ne, D, F),
                  p["w_expert_up"][l].reshape(ne, D, F), tm=tm)
    ys = _down(plan, hid, p["w_expert_down"][l].reshape(ne, F, D), tm=tm)
    return _combine(pos[0], pos[1], ys, x1, wts[0], wts[1], tc=ts["combine_tc"])


def kernel(x, mix_norm, w_in, diff_q_norm, diff_k_norm, diff_lambda, diff_subln, na_q_norm, na_k_norm, na_rpb, w_diff_out, w_na_out, w_out, ffn_norm, w_router_group, b_router_group, w_router_expert, b_router_expert, w_expert_gate, w_expert_up, w_expert_down):
    B, S, D = x.shape
    assert S % GRID_W == 0 and S % 128 == 0 and QKV_W % D == 0
    p = dict(mix_norm=mix_norm, w_in=w_in, diff_q_norm=diff_q_norm, diff_k_norm=diff_k_norm,
             diff_lambda=diff_lambda, diff_subln=diff_subln, na_q_norm=na_q_norm,
             na_k_norm=na_k_norm, na_rpb=na_rpb, w_diff_out=w_diff_out, w_na_out=w_na_out,
             w_out=w_out, ffn_norm=ffn_norm, w_router_group=w_router_group,
             b_router_group=b_router_group, w_router_expert=w_router_expert,
             b_router_expert=b_router_expert, w_expert_gate=w_expert_gate,
             w_expert_up=w_expert_up, w_expert_down=w_expert_down)
    x2 = x.reshape(B * S, D).astype(F32)
    for l in range(mix_norm.shape[0]):
        x2 = _layer(x2, p, l, batch=B, seq=S)
    return x2.reshape(B, S, D).astype(x.dtype)
```

```python
import functools
import math

import jax
import jax.numpy as jnp
import numpy as np
from jax import lax
from jax.experimental import pallas as pl
from jax.experimental.pallas import tpu as pltpu

F32 = jnp.float32
BF16 = jnp.bfloat16
I32 = jnp.int32

EPS = 1e-6
LOG2E = math.log2(math.e)
GRID_W = 64
DIFF_HEADS = 8
DIFF_QK_DIM = 64
NA_HEADS = 8
NA_DIM = 128
NA_WIN_ROWS = 8
NA_WIN_COLS = 16
TOP_K = 2
HEAD_W = 128
DQ_BLK, DK_BLK, DV_BLK = 0, DIFF_HEADS, 2 * DIFF_HEADS
NQ_BLK = 3 * DIFF_HEADS
NK_BLK = NQ_BLK + NA_HEADS
NV_BLK = NQ_BLK + 2 * NA_HEADS
QKV_W = (3 * DIFF_HEADS + 3 * NA_HEADS) * HEAD_W

ROUTER_ROWS = 48
VMEM_LIMIT = 56 * 1024 * 1024

_NT = (((1,), (1,)), ((), ()))


def _cparams(sem, vmem=VMEM_LIMIT):
    return pltpu.CompilerParams(dimension_semantics=sem, vmem_limit_bytes=vmem)


def _inproj_body(x_ref, g_ref, w_ref, o_ref, h_scr, *, rows):
    @pl.when(pl.program_id(1) == 0)
    def _():
        def chunk(c, carry):
            r = pl.ds(pl.multiple_of(c * rows, rows), rows)
            x = x_ref[r, :]
            ms = jnp.mean(x * x, axis=-1, keepdims=True)
            h_scr[r, :] = (x * lax.rsqrt(ms + EPS) * g_ref[...]).astype(BF16)
            return carry
        lax.fori_loop(0, x_ref.shape[0] // rows, chunk, 0)

    o_ref[...] = jnp.dot(h_scr[...], w_ref[...], preferred_element_type=F32).astype(BF16)


def _inproj(x2, gain, w_bf16, *, tm, tn):
    T, D = x2.shape
    N = w_bf16.shape[1]
    return pl.pallas_call(
        functools.partial(_inproj_body, rows=128),
        grid=(T // tm, N // tn),
        in_specs=[pl.BlockSpec((tm, D), lambda i, j: (i, 0)),
                  pl.BlockSpec((1, D), lambda i, j: (0, 0)),
                  pl.BlockSpec((D, tn), lambda i, j: (0, j))],
        out_specs=pl.BlockSpec((tm, tn), lambda i, j: (i, j)),
        out_shape=jax.ShapeDtypeStruct((T, N), BF16),
        scratch_shapes=[pltpu.VMEM((tm, D), BF16)],
        compiler_params=_cparams(("parallel", "arbitrary")),
        name="inproj",
    )(x2, gain.reshape(1, D), w_bf16)


def _half_rms(x, gain, first):
    x2 = x * x
    s0 = jnp.sum(jnp.where(first, x2, 0.0), axis=-1, keepdims=True)
    s1 = jnp.sum(jnp.where(first, 0.0, x2), axis=-1, keepdims=True)
    inv = jnp.where(first, lax.rsqrt(s0 * (1.0 / DIFF_QK_DIM) + EPS),
                    lax.rsqrt(s1 * (1.0 / DIFF_QK_DIM) + EPS))
    return x * inv * gain


def _alibi_tiles(slopes, seq):
    nb = seq // HEAD_W
    off = np.arange(HEAD_W)
    delta = (np.arange(-(nb - 1), nb)[:, None, None] * HEAD_W + off[None, None, :] - off[None, :, None])
    dist = jnp.asarray(np.abs(delta).astype(np.float32))
    return (-LOG2E * slopes.astype(F32))[:, None, None, None] * dist


def _diff_body(lam_ref, q_ref, k_ref, v_ref, qg_ref, kg_ref, sg_ref, bias_ref, o_ref,
               qm_scr, kn_scr, vt_scr, s0_scr, s1_scr, p_scr, ot_scr, *, tq, seq, out_scale):
    nkb = seq // HEAD_W
    nq = seq // tq
    qblks = tq // HEAD_W
    groups = HEAD_W // 8
    lane = lax.broadcasted_iota(I32, (1, HEAD_W), 1)
    first = lane < DIFF_QK_DIM
    kn_scr[...] = _half_rms(k_ref[...].astype(F32), kg_ref[...], first).astype(BF16)
    vt_scr[:HEAD_W, :] = v_ref[...].astype(F32).T.astype(BF16)
    vt_scr[HEAD_W:, :] = jnp.ones((8, seq), BF16)
    q = _half_rms(q_ref[...].astype(F32), qg_ref[...], first) * (DIFF_QK_DIM ** -0.5 * LOG2E)
    qm_scr[0] = jnp.where(first, q, 0.0).astype(BF16)
    qm_scr[1] = jnp.where(first, 0.0, q).astype(BF16)
    lam = lam_ref[0]

    def stage(score, expo):
        mx = [jnp.full((8, tq), -jnp.inf, F32), jnp.full((8, tq), -jnp.inf, F32)]
        if score is not None:
            qa, dst = score
            rows = pl.ds(pl.multiple_of(qa * tq, tq), tq)
            for m in range(2):
                s_all = lax.dot_general(kn_scr[...], qm_scr[m, rows, :], _NT, preferred_element_type=F32)
                for c in range(nkb):
                    ks = slice(c * HEAD_W, (c + 1) * HEAD_W)
                    tile0 = qa * qblks - c + (nkb - 1)
                    bias = jnp.concatenate([bias_ref[0, tile0 + t] for t in range(qblks)], axis=1)
                    s = s_all[ks, :] + bias
                    dst[m, ks, :] = s
                    mx[m] = jnp.maximum(mx[m], jnp.max(s.reshape(groups, 8, tq), axis=0))
        for c in range(nkb):
            ks = slice(c * HEAD_W, (c + 1) * HEAD_W)
            if expo is not None:
                src, mrow, buf = expo
                for m in range(2):
                    p_scr[buf, m, ks, :] = jnp.exp2((src[m, ks, :] - mrow[m]).astype(BF16))
        return tuple(jnp.max(a, axis=0, keepdims=True) for a in mx)

    def values(qb, buf):
        outs = []
        for m in range(2):
            ol = jnp.dot(vt_scr[...], p_scr[buf, m], preferred_element_type=F32)
            outs.append(ol[:HEAD_W] / ol[HEAD_W:HEAD_W + 1])
        ot_scr[qb] = outs[0] - lam * outs[1]

    mrow = stage((0, s0_scr), None)

    def pair(j, mrow):
        mrow1 = stage((2 * j + 1, s1_scr), (s0_scr, mrow, 0))
        values(2 * j, 0)
        mrow0 = stage((2 * j + 2, s0_scr), (s1_scr, mrow1, 1))
        values(2 * j + 1, 1)
        return mrow0
    mrow = lax.fori_loop(0, nq // 2 - 1, pair, mrow)
    mrow1 = stage((nq - 1, s1_scr), (s0_scr, mrow, 0))
    values(nq - 2, 0)
    stage(None, (s1_scr, mrow1, 1))
    values(nq - 1, 1)

    gain = sg_ref[...] * out_scale
    for i in range(nq):
        o = ot_scr[i]
        o = o * lax.rsqrt(jnp.mean(o * o, axis=0, keepdims=True) + EPS)
        o_ref[i * tq:(i + 1) * tq, :] = (o.T * gain).astype(BF16)


def _diff_attention(proj, bias_tiles, lam, q_gain, k_gain, sub_gain, *, batch, seq, tq, out_scale):
    T = proj.shape[0]
    smem = pl.BlockSpec(memory_space=pltpu.SMEM)
    vec = pl.BlockSpec((1, HEAD_W), lambda h, b: (0, 0))
    blk = lambda col: pl.BlockSpec((seq, HEAD_W), lambda h, b: (b, col + h))
    return pl.pallas_call(
        functools.partial(_diff_body, tq=tq, seq=seq, out_scale=out_scale),
        grid=(DIFF_HEADS, batch),
        in_specs=[smem, blk(DQ_BLK), blk(DK_BLK), blk(DV_BLK), vec, vec, vec,
                  pl.BlockSpec((1,) + bias_tiles.shape[1:], lambda h, b: (h, 0, 0, 0))],
        out_specs=pl.BlockSpec((seq, HEAD_W), lambda h, b: (b, h)),
        out_shape=jax.ShapeDtypeStruct((T, DIFF_HEADS * HEAD_W), BF16),
        scratch_shapes=[pltpu.VMEM((2, seq, HEAD_W), BF16), pltpu.VMEM((seq, HEAD_W), BF16),
                        pltpu.VMEM((HEAD_W + 8, seq), BF16),
                        pltpu.VMEM((2, seq, tq), F32), pltpu.VMEM((2, seq, tq), F32),
                        pltpu.VMEM((2, 2, seq, tq), BF16), pltpu.VMEM((seq // tq, HEAD_W, tq), F32)],
        compiler_params=_cparams(("parallel", "parallel")),
        name="diff_attn",
    )(lam, proj, proj, proj, q_gain, k_gain, sub_gain, bias_tiles)


NA_GROUP_ROWS = 4


def _na_plan(rows):
    wr = min(NA_WIN_ROWS, rows)
    gs = NA_GROUP_ROWS if rows % NA_GROUP_ROWS == 0 else 1
    ww = wr + gs - 1
    ww = min(ww + ww % 2, rows)
    row_start = lambda r: min(max(r - wr // 2, 0), rows - wr)
    groups, patterns = [], {}
    for g in range(rows // gs):
        ws = min(row_start(g * gs), rows - ww)
        key = tuple((row_start(r) - ws, r - ws) for r in range(g * gs, (g + 1) * gs))
        groups.append((ws, patterns.setdefault(key, len(patterns))))
    return wr, gs, ww, groups, list(patterns)


def _na_bias_table(rpb, rows):
    wr, gs, ww, _, patterns = _na_plan(rows)
    H = rpb.shape[0]
    c = np.arange(GRID_W)
    col_start = np.clip(c - NA_WIN_COLS // 2, 0, GRID_W - NA_WIN_COLS)
    col_in = (c[None, :] >= col_start[:, None]) & (c[None, :] < col_start[:, None] + NA_WIN_COLS)
    dc = np.clip(c[None, :] - c[:, None], -(NA_WIN_COLS - 1), NA_WIN_COLS - 1) + (NA_WIN_COLS - 1)
    sel = jnp.asarray((dc[:, :, None] == np.arange(2 * NA_WIN_COLS - 1)[None, None, :]).astype(np.float32))
    scaled = rpb.astype(F32) * LOG2E
    pats = []
    for key in patterns:
        blocks = []
        for win_off, row_off in key:
            lo = win_off - row_off + NA_WIN_ROWS - 1
            vals = jnp.einsum("hwd,ckd->hcwk", scaled[:, lo:lo + wr, :], sel,
                              precision=lax.Precision.HIGHEST)
            vals = jnp.where(jnp.asarray(col_in)[None, :, None, :], vals, -jnp.inf)
            pad = lambda n: jnp.full((H, GRID_W, n, GRID_W), -jnp.inf, F32)
            full = jnp.concatenate([pad(win_off), vals, pad(ww - win_off - wr)], axis=2)
            blocks.append(full.reshape(H, GRID_W, ww * GRID_W))
        pats.append(jnp.concatenate(blocks, axis=1))
    return jnp.stack(pats, axis=1)


def _na_body(q_ref, k_ref, v_ref, qg_ref, kg_ref, bias_ref, o_ref, qn_scr, kn_scr, va_scr, *, rows):
    wr, gs, ww, groups, _ = _na_plan(rows)

    def rms(x, g):
        return x * lax.rsqrt(jnp.mean(x * x, axis=-1, keepdims=True) + EPS) * g

    qn_scr[...] = (rms(q_ref[...].astype(F32), qg_ref[...]) * (NA_DIM ** -0.5 * LOG2E)).astype(BF16)
    kn_scr[...] = rms(k_ref[...].astype(F32), kg_ref[...]).astype(BF16)
    va_scr[:, :NA_DIM] = v_ref[...]
    va_scr[:, NA_DIM:] = jnp.ones((v_ref.shape[0], NA_DIM), BF16)

    for g, (ws, pid) in enumerate(groups):
        qs = slice(g * gs * GRID_W, (g + 1) * gs * GRID_W)
        ks = slice(ws * GRID_W, (ws + ww) * GRID_W)
        s = lax.dot_general(qn_scr[qs, :], kn_scr[ks, :], _NT, preferred_element_type=F32)
        s = s + bias_ref[0, pid]
        m = jnp.max(s, axis=-1, keepdims=True)
        p = jnp.exp2((s - m).astype(BF16))
        ol = jnp.dot(p, va_scr[ks, :], preferred_element_type=F32)
        o_ref[qs, :] = (ol[:, :NA_DIM] / ol[:, NA_DIM:]).astype(BF16)


def _na_attention(proj, q_gain, k_gain, bias_tab, *, batch, seq):
    T = proj.shape[0]
    rows = seq // GRID_W
    vec = pl.BlockSpec((1, NA_DIM), lambda h, b: (0, 0))
    return pl.pallas_call(
        functools.partial(_na_body, rows=rows),
        grid=(NA_HEADS, batch),
        in_specs=[pl.BlockSpec((seq, HEAD_W), lambda h, b: (b, NQ_BLK + h)),
                  pl.BlockSpec((seq, HEAD_W), lambda h, b: (b, NK_BLK + h)),
                  pl.BlockSpec((seq, HEAD_W), lambda h, b: (b, NV_BLK + h)),
                  vec, vec,
                  pl.BlockSpec((1,) + bias_tab.shape[1:], lambda h, b: (h, 0, 0, 0))],
        out_specs=pl.BlockSpec((seq, HEAD_W), lambda h, b: (b, h)),
        out_shape=jax.ShapeDtypeStruct((T, NA_HEADS * NA_DIM), BF16),
        scratch_shapes=[pltpu.VMEM((seq, HEAD_W), BF16), pltpu.VMEM((seq, HEAD_W), BF16),
                        pltpu.VMEM((seq, 2 * NA_DIM), BF16)],
        compiler_params=_cparams(("parallel", "parallel")),
        name="na_attn",
    )(proj, proj, proj, q_gain, k_gain, bias_tab)


def _outproj_body(oa_ref, ob_ref, ga_ref, gb_ref, x_ref, wd_ref, wn_ref, wo_ref, fg_ref, wr_ref,
                  x1_ref, lt_ref, m_scr, *, sub, ncol):
    tm, D = x_ref.shape
    for r in range(tm // sub):
        rs = slice(r * sub, (r + 1) * sub)
        oa = oa_ref[rs, :]
        ob = ob_ref[rs, :]
        for c in range(D // ncol):
            cs = slice(c * ncol, (c + 1) * ncol)
            ya = jnp.dot(oa, wd_ref[:, cs], preferred_element_type=F32)
            yb = jnp.dot(ob, wn_ref[:, cs], preferred_element_type=F32)
            m_scr[rs, cs] = (jax.nn.sigmoid(ga_ref[rs, cs].astype(F32)) * ya
                             + jax.nn.sigmoid(gb_ref[rs, cs].astype(F32)) * yb).astype(BF16)
        x1 = x_ref[rs, :] + jnp.dot(m_scr[rs, :], wo_ref[...], preferred_element_type=F32)
        x1_ref[rs, :] = x1
        h2 = x1 * lax.rsqrt(jnp.mean(x1 * x1, axis=-1, keepdims=True) + EPS) * fg_ref[...]
        h_hi = h2.astype(BF16)
        h_lo = (h2 - h_hi.astype(F32)).astype(BF16)
        both = lax.dot_general(wr_ref[...], h_hi, _NT, preferred_element_type=F32)
        cross = lax.dot_general(wr_ref[:ROUTER_ROWS, :], h_lo, _NT, preferred_element_type=F32)
        lt_ref[:, rs] = both[:ROUTER_ROWS] + both[ROUTER_ROWS:] + cross


def _outproj(oa, ob, proj, x2, wd, wn, wo, ffn_gain, wr_t, *, tm):
    T, D = x2.shape
    gate_blk = QKV_W // D
    const = lambda shape: pl.BlockSpec(shape, lambda i: (0, 0), pipeline_mode=pl.Buffered(1))
    return pl.pallas_call(
        functools.partial(_outproj_body, sub=min(tm, 256), ncol=min(D, 1024)),
        grid=(T // tm,),
        in_specs=[pl.BlockSpec((tm, oa.shape[1]), lambda i: (i, 0)),
                  pl.BlockSpec((tm, ob.shape[1]), lambda i: (i, 0)),
                  pl.BlockSpec((tm, D), lambda i: (i, gate_blk)),
                  pl.BlockSpec((tm, D), lambda i: (i, gate_blk + 1)),
                  pl.BlockSpec((tm, D), lambda i: (i, 0)),
                  const(wd.shape), const(wn.shape), const(wo.shape),
                  const((1, D)), const(wr_t.shape)],
        out_specs=[pl.BlockSpec((tm, D), lambda i: (i, 0)),
                   pl.BlockSpec((ROUTER_ROWS, tm), lambda i: (0, i))],
        out_shape=[jax.ShapeDtypeStruct((T, D), F32),
                   jax.ShapeDtypeStruct((ROUTER_ROWS, T), F32)],
        scratch_shapes=[pltpu.VMEM((tm, D), BF16)],
        compiler_params=_cparams(("parallel",)),
        name="outproj",
    )(oa, ob, proj, proj, x2, wd, wn, wo, ffn_gain.reshape(1, D), wr_t)


def _route_body(lt_ref, bias_ref, pos_ref, wts_ref, meta_ref, cnt_scr, carry_scr,
                *, n_groups, n_exp, tm, tc):
    ne = n_groups * n_exp
    phase = pl.program_id(0)
    step = pl.program_id(1)

    l = lt_ref[...] + bias_ref[...]
    el = l[0:ne]
    gl = l[ne:ne + n_groups]
    gmax = jnp.max(gl, axis=0, keepdims=True)
    gi = lax.broadcasted_iota(I32, gl.shape, 0)
    gsel = jnp.min(jnp.where(gl == gmax, gi, n_groups), axis=0, keepdims=True)
    gw = 1.0 / jnp.sum(jnp.exp(gl - gmax), axis=0, keepdims=True)
    ei = lax.broadcasted_iota(I32, el.shape, 0)
    lo = gsel * n_exp
    elm = jnp.where((ei >= lo) & (ei < lo + n_exp), el, -jnp.inf)
    m1 = jnp.max(elm, axis=0, keepdims=True)
    i1 = jnp.min(jnp.where(elm == m1, ei, ne), axis=0, keepdims=True)
    elm2 = jnp.where(ei == i1, -jnp.inf, elm)
    m2 = jnp.max(elm2, axis=0, keepdims=True)
    i2 = jnp.min(jnp.where(elm2 == m2, ei, ne), axis=0, keepdims=True)
    r = jnp.exp(m2 - m1)
    w1 = gw / (1.0 + r)
    w2 = w1 * r
    oh1 = ei == i1
    oh2 = ei == i2
    both = jnp.where(oh1 | oh2, 1.0, 0.0)

    @pl.when((phase == 0) & (step == 0))
    def _():
        cnt_scr[...] = jnp.zeros_like(cnt_scr)
        carry_scr[...] = jnp.zeros_like(carry_scr)

    @pl.when(phase == 0)
    def _():
        cnt_scr[...] += jnp.sum(both, axis=1, keepdims=True)

    @pl.when(phase == 1)
    def _():
        cnt = cnt_scr[...]
        tiles = jnp.floor((cnt + (tm - 1)) * (1.0 / tm))
        ea = lax.broadcasted_iota(I32, (ne, ne), 0)
        eb = lax.broadcasted_iota(I32, (ne, ne), 1)
        lower = jnp.where(eb < ea, 1.0, 0.0).astype(BF16)
        tiles_b = jnp.broadcast_to(tiles, (ne, 128)).astype(BF16)
        start_t = jnp.dot(lower, tiles_b, preferred_element_type=F32)[:, 0:1]
        ta = lax.broadcasted_iota(I32, (tc, tc), 0)
        tb = lax.broadcasted_iota(I32, (tc, tc), 1)
        upper = jnp.where(ta < tb, 1.0, 0.0).astype(BF16)
        prefix = jnp.dot(both.astype(BF16), upper, preferred_element_type=F32)
        base = prefix + carry_scr[...] + start_t * tm
        pos1 = jnp.sum(jnp.where(oh1, base, 0.0), axis=0, keepdims=True)
        pos2 = jnp.sum(jnp.where(oh2, base, 0.0), axis=0, keepdims=True)
        carry_scr[...] += jnp.sum(both, axis=1, keepdims=True)
        pos_ref[...] = jnp.zeros_like(pos_ref)
        pos_ref[0:1, :] = pos1.astype(I32)
        pos_ref[1:2, :] = pos2.astype(I32)
        wts_ref[...] = jnp.zeros_like(wts_ref)
        wts_ref[0:1, :] = w1
        wts_ref[1:2, :] = w2

        nt = meta_ref.shape[1]
        tj = lax.broadcasted_iota(I32, (ne, nt), 1).astype(F32)
        end_t = start_t + tiles
        texp = jnp.sum(jnp.where(end_t <= tj, 1.0, 0.0), axis=0, keepdims=True)
        is_first = jnp.sum(jnp.where((start_t == tj) & (tiles > 0.0), 1.0, 0.0),
                           axis=0, keepdims=True)
        is_last = jnp.sum(jnp.where((end_t - 1.0 == tj) & (tiles > 0.0), 1.0, 0.0),
                          axis=0, keepdims=True)
        nused = jnp.broadcast_to(jnp.sum(tiles, axis=0, keepdims=True), (1, nt))
        meta_ref[...] = jnp.zeros_like(meta_ref)
        meta_ref[0:1, :] = jnp.minimum(texp, ne - 1.0).astype(I32)
        meta_ref[1:2, :] = is_first.astype(I32)
        fill = is_last + jnp.where(tj[0:1, :] >= nused, 1.0, 0.0)
        meta_ref[2:3, :] = fill.astype(I32)
        meta_ref[3:4, :] = nused.astype(I32)
        used = tiles > 0.0
        seg = jnp.sum(jnp.where((end_t <= tj) & used, 1.0, 0.0), axis=0, keepdims=True)
        eidx = lax.broadcasted_iota(I32, (ne, nt), 0).astype(F32)
        nxt = jnp.min(jnp.where((start_t > tj) & used, eidx, float(ne)), axis=0, keepdims=True)
        meta_ref[4:5, :] = (seg - 2.0 * jnp.floor(seg * 0.5)).astype(I32)
        meta_ref[5:6, :] = jnp.where(nxt >= ne, -1.0, nxt).astype(I32)


def _route(lt, bias_col, *, n_groups, n_exp, tm, n_tiles, tc):
    T = lt.shape[1]
    ne = n_groups * n_exp
    nt_pad = -(-n_tiles // 128) * 128
    return pl.pallas_call(
        functools.partial(_route_body, n_groups=n_groups, n_exp=n_exp, tm=tm, tc=tc),
        grid=(2, T // tc),
        in_specs=[pl.BlockSpec((ROUTER_ROWS, tc), lambda p, i: (0, i)),
                  pl.BlockSpec((ROUTER_ROWS, 1), lambda p, i: (0, 0))],
        out_specs=[pl.BlockSpec((8, tc), lambda p, i: (0, i * p)),
                   pl.BlockSpec((8, tc), lambda p, i: (0, i * p)),
                   pl.BlockSpec((8, nt_pad), lambda p, i: (0, 0))],
        out_shape=[jax.ShapeDtypeStruct((8, T), I32),
                   jax.ShapeDtypeStruct((8, T), F32),
                   jax.ShapeDtypeStruct((8, nt_pad), I32)],
        scratch_shapes=[pltpu.VMEM((ne, 1), F32), pltpu.VMEM((ne, 1), F32)],
        compiler_params=_cparams(("arbitrary", "arbitrary")),
        name="route",
    )(lt, bias_col)


DMA_UNROLL = 8


def _dispatch_body(fill_ref, pos1_ref, pos2_ref, h_ref, xs_ref, zero_scr, sem, zsem, *, td, tm):
    n_tiles = fill_ref.shape[0]

    def zero_copy(j):
        return pltpu.make_async_copy(zero_scr, xs_ref.at[pl.ds(pl.multiple_of(j * tm, tm), tm), :], zsem)

    @pl.when(pl.program_id(0) == 0)
    def _():
        zero_scr[...] = jnp.zeros_like(zero_scr)

        def start(j, carry):
            @pl.when(fill_ref[j] == 1)
            def _():
                zero_copy(j).start()
            return carry
        lax.fori_loop(0, n_tiles, start, 0)

        def wait(j, carry):
            @pl.when(fill_ref[j] == 1)
            def _():
                zero_copy(j).wait()
            return carry
        lax.fori_loop(0, n_tiles, wait, 0)

    def row_copy(t, pos):
        return pltpu.make_async_copy(h_ref.at[pl.ds(t, 1), :], xs_ref.at[pl.ds(pos, 1), :], sem)

    def issue(t, carry):
        row_copy(t, pos1_ref[0, 0, t]).start()
        row_copy(t, pos2_ref[0, 0, t]).start()
        return carry
    lax.fori_loop(0, td, issue, 0, unroll=DMA_UNROLL)

    def drain(t, carry):
        row_copy(t, 0).wait()
        row_copy(t, 0).wait()
        return carry
    lax.fori_loop(0, td, drain, 0, unroll=DMA_UNROLL)


def _dispatch(fill, pos1, pos2, rows_in, *, n_rows, td, tm):
    T, W = rows_in.shape
    nblk = T // td
    smem3 = pl.BlockSpec((1, 1, td), lambda i, fl: (i, 0, 0), memory_space=pltpu.SMEM)
    grid_spec = pltpu.PrefetchScalarGridSpec(
        num_scalar_prefetch=1,
        grid=(nblk,),
        in_specs=[smem3, smem3, pl.BlockSpec((td, W), lambda i, fl: (i, 0))],
        out_specs=pl.BlockSpec(memory_space=pl.ANY),
        scratch_shapes=[pltpu.VMEM((tm, W), rows_in.dtype), pltpu.SemaphoreType.DMA(()),
                        pltpu.SemaphoreType.DMA(())],
    )
    return pl.pallas_call(
        functools.partial(_dispatch_body, td=td, tm=tm),
        grid_spec=grid_spec,
        out_shape=jax.ShapeDtypeStruct((n_rows, W), rows_in.dtype),
        compiler_params=_cparams(("arbitrary",)),
        name="dispatch",
    )(fill, pos1.reshape(nblk, 1, td), pos2.reshape(nblk, 1, td), rows_in)


def _expert_weights(plan, j, hbm_refs, buf_refs, bf16_refs, sem):
    texp_ref, first_ref, slot_ref, next_ref = plan

    def copies(e, slot):
        return [pltpu.make_async_copy(h.at[e], b.at[slot], sem.at[i, slot])
                for i, (h, b) in enumerate(zip(hbm_refs, buf_refs))]

    @pl.when(j == 0)
    def _():
        for c in copies(texp_ref[0], 0):
            c.start()

    @pl.when(first_ref[j] == 1)
    def _():
        slot = slot_ref[j]
        for c in copies(texp_ref[j], slot):
            c.wait()

        @pl.when(next_ref[j] >= 0)
        def _():
            for c in copies(next_ref[j], 1 - slot):
                c.start()
        for b, w in zip(buf_refs, bf16_refs):
            w[...] = b[slot].astype(BF16)


def _gateup_body(texp_ref, first_ref, nused_ref, slot_ref, next_ref, x_ref, fg_ref, wg_hbm, wu_hbm,
                 hid_ref, wg_buf, wu_buf, wg_scr, wu_scr, sem):
    j = pl.program_id(0)

    @pl.when(j < nused_ref[0])
    def _():
        _expert_weights((texp_ref, first_ref, slot_ref, next_ref), j, (wg_hbm, wu_hbm),
                        (wg_buf, wu_buf), (wg_scr, wu_scr), sem)
        x = x_ref[...]
        x = (x * lax.rsqrt(jnp.mean(x * x, axis=-1, keepdims=True) + EPS) * fg_ref[...]).astype(BF16)
        g = jnp.dot(x, wg_scr[...], preferred_element_type=F32)
        u = jnp.dot(x, wu_scr[...], preferred_element_type=F32)
        hid_ref[...] = (g * jax.nn.sigmoid(g) * u).astype(BF16)

    @pl.when(j >= nused_ref[0])
    def _():
        hid_ref[...] = jnp.zeros_like(hid_ref)


def _used_tile(j, nused):
    return jnp.minimum(j, nused[0] - 1)


def _gateup(plan, xs, ffn_gain, wg, wu, *, tm):
    P, W = xs.shape
    D, F = wg.shape[1:]
    hbm = pl.BlockSpec(memory_space=pl.ANY)
    grid_spec = pltpu.PrefetchScalarGridSpec(
        num_scalar_prefetch=5,
        grid=(P // tm,),
        in_specs=[pl.BlockSpec((tm, W), lambda j, te, fi, nu, sl, nx: (_used_tile(j, nu), 0)),
                  pl.BlockSpec((1, W), lambda j, te, fi, nu, sl, nx: (0, 0)), hbm, hbm],
        out_specs=pl.BlockSpec((tm, F), lambda j, te, fi, nu, sl, nx: (j, 0)),
        scratch_shapes=[pltpu.VMEM((2, D, F), wg.dtype), pltpu.VMEM((2, D, F), wu.dtype),
                        pltpu.VMEM((D, F), BF16), pltpu.VMEM((D, F), BF16),
                        pltpu.SemaphoreType.DMA((2, 2))],
    )
    return pl.pallas_call(
        _gateup_body,
        grid_spec=grid_spec,
        out_shape=jax.ShapeDtypeStruct((P, F), BF16),
        compiler_params=_cparams(("arbitrary",)),
        name="expert_gateup",
    )(*plan, xs, ffn_gain.reshape(1, W), wg, wu)


def _down_body(texp_ref, first_ref, nused_ref, slot_ref, next_ref, hid_ref, wd_hbm, y_ref,
               wd_buf, wd_scr, sem):
    j = pl.program_id(0)

    @pl.when(j < nused_ref[0])
    def _():
        _expert_weights((texp_ref, first_ref, slot_ref, next_ref), j, (wd_hbm,), (wd_buf,),
                        (wd_scr,), sem)
        y_ref[...] = jnp.dot(hid_ref[...], wd_scr[...], preferred_element_type=F32)

    @pl.when(j >= nused_ref[0])
    def _():
        y_ref[...] = jnp.zeros_like(y_ref)


def _down(plan, hid, wd, *, tm):
    P, F = hid.shape
    D = wd.shape[2]
    grid_spec = pltpu.PrefetchScalarGridSpec(
        num_scalar_prefetch=5,
        grid=(P // tm,),
        in_specs=[pl.BlockSpec((tm, F), lambda j, te, fi, nu, sl, nx: (_used_tile(j, nu), 0)),
                  pl.BlockSpec(memory_space=pl.ANY)],
        out_specs=pl.BlockSpec((tm, D), lambda j, te, fi, nu, sl, nx: (j, 0)),
        scratch_shapes=[pltpu.VMEM((2, F, D), wd.dtype), pltpu.VMEM((F, D), BF16),
                        pltpu.SemaphoreType.DMA((1, 2))],
    )
    return pl.pallas_call(
        _down_body,
        grid_spec=grid_spec,
        out_shape=jax.ShapeDtypeStruct((P, D), F32),
        compiler_params=_cparams(("arbitrary",)),
        name="expert_down",
    )(*plan, hid, wd)


def _combine_body(pos1_ref, pos2_ref, nxt1_ref, nxt2_ref, ys_ref, x1_ref, w1_ref, w2_ref, o_ref,
                  r1_scr, r2_scr, sem, *, tc):
    step = pl.program_id(0)
    slot = step % 2

    def row_copy(pos, dst, s, t):
        return pltpu.make_async_copy(ys_ref.at[pl.ds(pos, 1), :], dst.at[s, pl.ds(t, 1), :], sem.at[s])

    def gather(p1_ref, p2_ref, s):
        def issue(t, carry):
            row_copy(p1_ref[0, 0, t], r1_scr, s, t).start()
            row_copy(p2_ref[0, 0, t], r2_scr, s, t).start()
            return carry
        lax.fori_loop(0, tc, issue, 0, unroll=DMA_UNROLL)

    @pl.when(step == 0)
    def _():
        gather(pos1_ref, pos2_ref, 0)

    @pl.when(step + 1 < pl.num_programs(0))
    def _():
        gather(nxt1_ref, nxt2_ref, 1 - slot)

    def drain(t, carry):
        row_copy(0, r1_scr, slot, t).wait()
        row_copy(0, r2_scr, slot, t).wait()
        return carry
    lax.fori_loop(0, tc, drain, 0, unroll=DMA_UNROLL)

    o_ref[...] = x1_ref[...] + w1_ref[...] * r1_scr[slot] + w2_ref[...] * r2_scr[slot]


def _combine(pos1, pos2, ys, x1, w1, w2, *, tc):
    T, D = x1.shape
    nblk = T // tc
    smem3 = pl.BlockSpec((1, 1, tc), lambda i: (i, 0, 0), memory_space=pltpu.SMEM)
    smem3_next = pl.BlockSpec((1, 1, tc), lambda i: (jnp.minimum(i + 1, nblk - 1), 0, 0),
                              memory_space=pltpu.SMEM)
    col = pl.BlockSpec((tc, 1), lambda i: (i, 0))
    pos1 = pos1.reshape(nblk, 1, tc)
    pos2 = pos2.reshape(nblk, 1, tc)
    return pl.pallas_call(
        functools.partial(_combine_body, tc=tc),
        grid=(nblk,),
        in_specs=[smem3, smem3, smem3_next, smem3_next, pl.BlockSpec(memory_space=pl.ANY),
                  pl.BlockSpec((tc, D), lambda i: (i, 0)), col, col],
        out_specs=pl.BlockSpec((tc, D), lambda i: (i, 0)),
        out_shape=jax.ShapeDtypeStruct((T, D), F32),
        scratch_shapes=[pltpu.VMEM((2, tc, D), F32), pltpu.VMEM((2, tc, D), F32),
                        pltpu.SemaphoreType.DMA((2,))],
        compiler_params=_cparams(("arbitrary",)),
        name="combine",
    )(pos1, pos2, pos1, pos2, ys, x1, w1.reshape(T, 1), w2.reshape(T, 1))


def _tile_sizes(T, D, seq, in_w):
    pick = lambda n, pref: pref if n % pref == 0 else n
    return dict(
        inproj_tm=pick(T, 1024), inproj_tn=next(t for t in (2048, 1024, 512, 256, 128) if in_w % t == 0),
        diff_tq=pick(seq, 256),
        outproj_tm=pick(T, 512),
        route_tc=pick(T, 1024),
        expert_tm=256,
        dispatch_td=pick(T, 1024), combine_tc=pick(T, 256),
    )


def _layer(x2, p, l, *, batch, seq):
    T, D = x2.shape
    ts = _tile_sizes(T, D, seq, p["w_in"].shape[-1])
    lambda_init = 0.8 - 0.6 * math.exp(-0.3 * l)
    n_groups, n_exp = p["b_router_expert"].shape[1:]
    ne = n_groups * n_exp
    rows = seq // GRID_W

    proj = _inproj(x2, p["mix_norm"][l], p["w_in"][l].astype(BF16),
                   tm=ts["inproj_tm"], tn=ts["inproj_tn"])

    lam = p["diff_lambda"][l].astype(F32)
    lam_full = (jnp.exp(jnp.sum(lam[0] * lam[1])) - jnp.exp(jnp.sum(lam[2] * lam[3]))
                + lambda_init).reshape(1)
    slopes = 2.0 ** (-8.0 * jnp.arange(1, DIFF_HEADS + 1, dtype=F32) / DIFF_HEADS)
    two = lambda g: jnp.tile(g.astype(F32), 2).reshape(1, HEAD_W)
    oa = _diff_attention(proj, _alibi_tiles(slopes, seq), lam_full, two(p["diff_q_norm"][l]), two(p["diff_k_norm"][l]),
                         p["diff_subln"][l].astype(F32).reshape(1, HEAD_W),
                         batch=batch, seq=seq, tq=ts["diff_tq"], out_scale=1.0 - lambda_init)

    ob = _na_attention(proj, p["na_q_norm"][l].astype(F32).reshape(1, NA_DIM),
                       p["na_k_norm"][l].astype(F32).reshape(1, NA_DIM),
                       _na_bias_table(p["na_rpb"][l], rows), batch=batch, seq=seq)

    wr_t = jnp.concatenate([p["w_router_expert"][l].T, p["w_router_group"][l].T,
                            jnp.zeros((ROUTER_ROWS - ne - n_groups, D), F32)], axis=0).astype(F32)
    wr_hi = wr_t.astype(BF16)
    wr_t = jnp.concatenate([wr_hi, (wr_t - wr_hi.astype(F32)).astype(BF16)], axis=0)
    b_col = jnp.concatenate([p["b_router_expert"][l].reshape(ne), p["b_router_group"][l],
                             jnp.zeros((ROUTER_ROWS - ne - n_groups,), F32)]).astype(F32).reshape(ROUTER_ROWS, 1)
    x1, lt = _outproj(oa, ob, proj, x2, p["w_diff_out"][l].astype(BF16),
                          p["w_na_out"][l].astype(BF16), p["w_out"][l].astype(BF16),
                          p["ffn_norm"][l].astype(F32), wr_t, tm=ts["outproj_tm"])

    tm = ts["expert_tm"]
    n_tiles = (T * TOP_K) // tm + ne
    pos, wts, meta = _route(lt, b_col, n_groups=n_groups, n_exp=n_exp, tm=tm, n_tiles=n_tiles,
                            tc=ts["route_tc"])
    fill = meta[2, :n_tiles]
    plan = (meta[0, :n_tiles], meta[1, :n_tiles], meta[3, :1], meta[4, :n_tiles], meta[5, :n_tiles])

    xs = _dispatch(fill, pos[0], pos[1], x1, n_rows=n_tiles * tm, td=ts["dispatch_td"], tm=tm)
    F = p["w_expert_gate"].shape[-1]
    hid = _gateup(plan, xs, p["ffn_norm"][l].astype(F32), p["w_expert_gate"][l].reshape(ne, D, F),
                  p["w_expert_up"][l].reshape(ne, D, F), tm=tm)
    ys = _down(plan, hid, p["w_expert_down"][l].reshape(ne, F, D), tm=tm)
    return _combine(pos[0], pos[1], ys, x1, wts[0], wts[1], tc=ts["combine_tc"])


def kernel(x, mix_norm, w_in, diff_q_norm, diff_k_norm, diff_lambda, diff_subln, na_q_norm, na_k_norm, na_rpb, w_diff_out, w_na_out, w_out, ffn_norm, w_router_group, b_router_group, w_router_expert, b_router_expert, w_expert_gate, w_expert_up, w_expert_down):
    B, S, D = x.shape
    assert S % GRID_W == 0 and S % 128 == 0 and QKV_W % D == 0
    p = dict(mix_norm=mix_norm, w_in=w_in, diff_q_norm=diff_q_norm, diff_k_norm=diff_k_norm,
             diff_lambda=diff_lambda, diff_subln=diff_subln, na_q_norm=na_q_norm,
             na_k_norm=na_k_norm, na_rpb=na_rpb, w_diff_out=w_diff_out, w_na_out=w_na_out,
             w_out=w_out, ffn_norm=ffn_norm, w_router_group=w_router_group,
             b_router_group=b_router_group, w_router_expert=w_router_expert,
             b_router_expert=b_router_expert, w_expert_gate=w_expert_gate,
             w_expert_up=w_expert_up, w_expert_down=w_expert_down)
    x2 = x.reshape(B * S, D).astype(F32)
    for l in range(mix_norm.shape[0]):
        x2 = _layer(x2, p, l, batch=B, seq=S)
    return x2.reshape(B, S, D).astype(x.dtype)
```

```python
import functools
import math

import jax
import jax.numpy as jnp
import numpy as np
from jax import lax
from jax.experimental import pallas as pl
from jax.experimental.pallas import tpu as pltpu

F32 = jnp.float32
BF16 = jnp.bfloat16
I32 = jnp.int32

EPS = 1e-6
LOG2E = math.log2(math.e)
GRID_W = 64
DIFF_HEADS = 8
DIFF_QK_DIM = 64
NA_HEADS = 8
NA_DIM = 128
NA_WIN_ROWS = 8
NA_WIN_COLS = 16
TOP_K = 2
HEAD_W = 128
DQ_BLK, DK_BLK, DV_BLK = 0, DIFF_HEADS, 2 * DIFF_HEADS
NQ_BLK = 3 * DIFF_HEADS
NK_BLK = NQ_BLK + NA_HEADS
NV_BLK = NQ_BLK + 2 * NA_HEADS
QKV_W = (3 * DIFF_HEADS + 3 * NA_HEADS) * HEAD_W

ROUTER_ROWS = 48
VMEM_LIMIT = 56 * 1024 * 1024

_NT = (((1,), (1,)), ((), ()))


def _cparams(sem, vmem=VMEM_LIMIT):
    return pltpu.CompilerParams(dimension_semantics=sem, vmem_limit_bytes=vmem)


def _inproj_body(x_ref, g_ref, w_ref, o_ref, h_scr, *, rows):
    @pl.when(pl.program_id(1) == 0)
    def _():
        def chunk(c, carry):
            r = pl.ds(pl.multiple_of(c * rows, rows), rows)
            x = x_ref[r, :]
            ms = jnp.mean(x * x, axis=-1, keepdims=True)
            h_scr[r, :] = (x * lax.rsqrt(ms + EPS) * g_ref[...]).astype(BF16)
            return carry
        lax.fori_loop(0, x_ref.shape[0] // rows, chunk, 0)

    o_ref[...] = jnp.dot(h_scr[...], w_ref[...], preferred_element_type=F32).astype(BF16)


def _inproj(x2, gain, w_bf16, *, tm, tn):
    T, D = x2.shape
    N = w_bf16.shape[1]
    return pl.pallas_call(
        functools.partial(_inproj_body, rows=128),
        grid=(T // tm, N // tn),
        in_specs=[pl.BlockSpec((tm, D), lambda i, j: (i, 0)),
                  pl.BlockSpec((1, D), lambda i, j: (0, 0)),
                  pl.BlockSpec((D, tn), lambda i, j: (0, j))],
        out_specs=pl.BlockSpec((tm, tn), lambda i, j: (i, j)),
        out_shape=jax.ShapeDtypeStruct((T, N), BF16),
        scratch_shapes=[pltpu.VMEM((tm, D), BF16)],
        compiler_params=_cparams(("parallel", "arbitrary")),
        name="inproj",
    )(x2, gain.reshape(1, D), w_bf16)


def _half_rms(x, gain, first):
    x2 = x * x
    s0 = jnp.sum(jnp.where(first, x2, 0.0), axis=-1, keepdims=True)
    s1 = jnp.sum(jnp.where(first, 0.0, x2), axis=-1, keepdims=True)
    inv = jnp.where(first, lax.rsqrt(s0 * (1.0 / DIFF_QK_DIM) + EPS),
                    lax.rsqrt(s1 * (1.0 / DIFF_QK_DIM) + EPS))
    return x * inv * gain


def _alibi_tiles(slopes, seq):
    nb = seq // HEAD_W
    off = np.arange(HEAD_W)
    delta = (np.arange(-(nb - 1), nb)[:, None, None] * HEAD_W + off[None, None, :] - off[None, :, None])
    dist = jnp.asarray(np.abs(delta).astype(np.float32))
    return (-LOG2E * slopes.astype(F32))[:, None, None, None] * dist


def _diff_body(lam_ref, q_ref, k_ref, v_ref, qg_ref, kg_ref, sg_ref, bias_ref, o_ref,
               qm_scr, kn_scr, vt_scr, s0_scr, s1_scr, p_scr, ot_scr, *, tq, seq, out_scale):
    nkb = seq // HEAD_W
    nq = seq // tq
    qblks = tq // HEAD_W
    groups = HEAD_W // 8
    lane = lax.broadcasted_iota(I32, (1, HEAD_W), 1)
    first = lane < DIFF_QK_DIM
    kn_scr[...] = _half_rms(k_ref[...].astype(F32), kg_ref[...], first).astype(BF16)
    vt_scr[:HEAD_W, :] = v_ref[...].astype(F32).T.astype(BF16)
    vt_scr[HEAD_W:, :] = jnp.ones((8, seq), BF16)
    q = _half_rms(q_ref[...].astype(F32), qg_ref[...], first) * (DIFF_QK_DIM ** -0.5 * LOG2E)
    qt = q.T
    top = lax.broadcasted_iota(I32, (HEAD_W, 1), 0) < DIFF_QK_DIM
    for i in range(nq):
        blk = slice(i * tq, (i + 1) * tq)
        qm_scr[0, i] = jnp.where(top, qt[:, blk], 0.0).astype(BF16)
        qm_scr[1, i] = jnp.where(top, 0.0, qt[:, blk]).astype(BF16)
    lam = lam_ref[0]

    def stage(score, expo):
        mx = [jnp.full((8, tq), -jnp.inf, F32), jnp.full((8, tq), -jnp.inf, F32)]
        if score is not None:
            qa, dst = score
            for m in range(2):
                s_all = jnp.dot(kn_scr[...], qm_scr[m, qa], preferred_element_type=F32)
                for c in range(nkb):
                    ks = slice(c * HEAD_W, (c + 1) * HEAD_W)
                    tile0 = qa * qblks - c + (nkb - 1)
                    bias = jnp.concatenate([bias_ref[0, tile0 + t] for t in range(qblks)], axis=1)
                    s = s_all[ks, :] + bias
                    dst[m, ks, :] = s
                    mx[m] = jnp.maximum(mx[m], jnp.max(s.reshape(groups, 8, tq), axis=0))
        for c in range(nkb):
            ks = slice(c * HEAD_W, (c + 1) * HEAD_W)
            if expo is not None:
                src, mrow, buf = expo
                for m in range(2):
                    p_scr[buf, m, ks, :] = jnp.exp2((src[m, ks, :] - mrow[m]).astype(BF16))
        return tuple(jnp.max(a, axis=0, keepdims=True) for a in mx)

    def values(qb, buf):
        outs = []
        for m in range(2):
            ol = jnp.dot(vt_scr[...], p_scr[buf, m], preferred_element_type=F32)
            outs.append(ol[:HEAD_W] / ol[HEAD_W:HEAD_W + 1])
        ot_scr[qb] = outs[0] - lam * outs[1]

    mrow = stage((0, s0_scr), None)

    def pair(j, mrow):
        mrow1 = stage((2 * j + 1, s1_scr), (s0_scr, mrow, 0))
        values(2 * j, 0)
        mrow0 = stage((2 * j + 2, s0_scr), (s1_scr, mrow1, 1))
        values(2 * j + 1, 1)
        return mrow0
    mrow = lax.fori_loop(0, nq // 2 - 1, pair, mrow)
    mrow1 = stage((nq - 1, s1_scr), (s0_scr, mrow, 0))
    values(nq - 2, 0)
    stage(None, (s1_scr, mrow1, 1))
    values(nq - 1, 1)

    gain = sg_ref[...] * out_scale
    for i in range(nq):
        o = ot_scr[i]
        o = o * lax.rsqrt(jnp.mean(o * o, axis=0, keepdims=True) + EPS)
        o_ref[i * tq:(i + 1) * tq, :] = (o.T * gain).astype(BF16)


def _diff_attention(proj, bias_tiles, lam, q_gain, k_gain, sub_gain, *, batch, seq, tq, out_scale):
    T = proj.shape[0]
    smem = pl.BlockSpec(memory_space=pltpu.SMEM)
    vec = pl.BlockSpec((1, HEAD_W), lambda h, b: (0, 0))
    blk = lambda col: pl.BlockSpec((seq, HEAD_W), lambda h, b: (b, col + h))
    return pl.pallas_call(
        functools.partial(_diff_body, tq=tq, seq=seq, out_scale=out_scale),
        grid=(DIFF_HEADS, batch),
        in_specs=[smem, blk(DQ_BLK), blk(DK_BLK), blk(DV_BLK), vec, vec, vec,
                  pl.BlockSpec((1,) + bias_tiles.shape[1:], lambda h, b: (h, 0, 0, 0))],
        out_specs=pl.BlockSpec((seq, HEAD_W), lambda h, b: (b, h)),
        out_shape=jax.ShapeDtypeStruct((T, DIFF_HEADS * HEAD_W), BF16),
        scratch_shapes=[pltpu.VMEM((2, seq // tq, HEAD_W, tq), BF16), pltpu.VMEM((seq, HEAD_W), BF16),
                        pltpu.VMEM((HEAD_W + 8, seq), BF16),
                        pltpu.VMEM((2, seq, tq), F32), pltpu.VMEM((2, seq, tq), F32),
                        pltpu.VMEM((2, 2, seq, tq), BF16), pltpu.VMEM((seq // tq, HEAD_W, tq), F32)],
        compiler_params=_cparams(("parallel", "parallel")),
        name="diff_attn",
    )(lam, proj, proj, proj, q_gain, k_gain, sub_gain, bias_tiles)


NA_GROUP_ROWS = 4


def _na_plan(rows):
    wr = min(NA_WIN_ROWS, rows)
    gs = NA_GROUP_ROWS if rows % NA_GROUP_ROWS == 0 else 1
    ww = wr + gs - 1
    ww = min(ww + ww % 2, rows)
    row_start = lambda r: min(max(r - wr // 2, 0), rows - wr)
    groups, patterns = [], {}
    for g in range(rows // gs):
        ws = min(row_start(g * gs), rows - ww)
        key = tuple((row_start(r) - ws, r - ws) for r in range(g * gs, (g + 1) * gs))
        groups.append((ws, patterns.setdefault(key, len(patterns))))
    return wr, gs, ww, groups, list(patterns)


def _na_bias_table(rpb, rows):
    wr, gs, ww, _, patterns = _na_plan(rows)
    H = rpb.shape[0]
    c = np.arange(GRID_W)
    col_start = np.clip(c - NA_WIN_COLS // 2, 0, GRID_W - NA_WIN_COLS)
    col_in = (c[None, :] >= col_start[:, None]) & (c[None, :] < col_start[:, None] + NA_WIN_COLS)
    dc = np.clip(c[None, :] - c[:, None], -(NA_WIN_COLS - 1), NA_WIN_COLS - 1) + (NA_WIN_COLS - 1)
    sel = jnp.asarray((dc[:, :, None] == np.arange(2 * NA_WIN_COLS - 1)[None, None, :]).astype(np.float32))
    scaled = rpb.astype(F32) * LOG2E
    pats = []
    for key in patterns:
        blocks = []
        for win_off, row_off in key:
            lo = win_off - row_off + NA_WIN_ROWS - 1
            vals = jnp.einsum("hwd,ckd->hcwk", scaled[:, lo:lo + wr, :], sel,
                              precision=lax.Precision.HIGHEST)
            vals = jnp.where(jnp.asarray(col_in)[None, :, None, :], vals, -jnp.inf)
            pad = lambda n: jnp.full((H, GRID_W, n, GRID_W), -jnp.inf, F32)
            full = jnp.concatenate([pad(win_off), vals, pad(ww - win_off - wr)], axis=2)
            blocks.append(full.reshape(H, GRID_W, ww * GRID_W))
        pats.append(jnp.concatenate(blocks, axis=1))
    return jnp.stack(pats, axis=1)


def _na_body(q_ref, k_ref, v_ref, qg_ref, kg_ref, bias_ref, o_ref, qn_scr, kn_scr, va_scr, *, rows):
    wr, gs, ww, groups, _ = _na_plan(rows)

    def rms(x, g):
        return x * lax.rsqrt(jnp.mean(x * x, axis=-1, keepdims=True) + EPS) * g

    qn_scr[...] = (rms(q_ref[...].astype(F32), qg_ref[...]) * (NA_DIM ** -0.5 * LOG2E)).astype(BF16)
    kn_scr[...] = rms(k_ref[...].astype(F32), kg_ref[...]).astype(BF16)
    va_scr[:, :NA_DIM] = v_ref[...]
    va_scr[:, NA_DIM:] = jnp.ones((v_ref.shape[0], NA_DIM), BF16)

    for g, (ws, pid) in enumerate(groups):
        qs = slice(g * gs * GRID_W, (g + 1) * gs * GRID_W)
        ks = slice(ws * GRID_W, (ws + ww) * GRID_W)
        s = lax.dot_general(qn_scr[qs, :], kn_scr[ks, :], _NT, preferred_element_type=F32)
        s = s + bias_ref[0, pid]
        m = jnp.max(s, axis=-1, keepdims=True)
        p = jnp.exp2((s - m).astype(BF16))
        ol = jnp.dot(p, va_scr[ks, :], preferred_element_type=F32)
        o_ref[qs, :] = (ol[:, :NA_DIM] / ol[:, NA_DIM:]).astype(BF16)


def _na_attention(proj, q_gain, k_gain, bias_tab, *, batch, seq):
    T = proj.shape[0]
    rows = seq // GRID_W
    vec = pl.BlockSpec((1, NA_DIM), lambda h, b: (0, 0))
    return pl.pallas_call(
        functools.partial(_na_body, rows=rows),
        grid=(NA_HEADS, batch),
        in_specs=[pl.BlockSpec((seq, HEAD_W), lambda h, b: (b, NQ_BLK + h)),
                  pl.BlockSpec((seq, HEAD_W), lambda h, b: (b, NK_BLK + h)),
                  pl.BlockSpec((seq, HEAD_W), lambda h, b: (b, NV_BLK + h)),
                  vec, vec,
                  pl.BlockSpec((1,) + bias_tab.shape[1:], lambda h, b: (h, 0, 0, 0))],
        out_specs=pl.BlockSpec((seq, HEAD_W), lambda h, b: (b, h)),
        out_shape=jax.ShapeDtypeStruct((T, NA_HEADS * NA_DIM), BF16),
        scratch_shapes=[pltpu.VMEM((seq, HEAD_W), BF16), pltpu.VMEM((seq, HEAD_W), BF16),
                        pltpu.VMEM((seq, 2 * NA_DIM), BF16)],
        compiler_params=_cparams(("parallel", "parallel")),
        name="na_attn",
    )(proj, proj, proj, q_gain, k_gain, bias_tab)


def _outproj_body(oa_ref, ob_ref, ga_ref, gb_ref, x_ref, wd_ref, wn_ref, wo_ref, fg_ref, wr_ref,
                  x1_ref, lt_ref, m_scr, *, sub, ncol):
    tm, D = x_ref.shape
    for r in range(tm // sub):
        rs = slice(r * sub, (r + 1) * sub)
        oa = oa_ref[rs, :]
        ob = ob_ref[rs, :]
        for c in range(D // ncol):
            cs = slice(c * ncol, (c + 1) * ncol)
            ya = jnp.dot(oa, wd_ref[:, cs], preferred_element_type=F32)
            yb = jnp.dot(ob, wn_ref[:, cs], preferred_element_type=F32)
            m_scr[rs, cs] = (jax.nn.sigmoid(ga_ref[rs, cs].astype(F32)) * ya
                             + jax.nn.sigmoid(gb_ref[rs, cs].astype(F32)) * yb).astype(BF16)
        x1 = x_ref[rs, :] + jnp.dot(m_scr[rs, :], wo_ref[...], preferred_element_type=F32)
        x1_ref[rs, :] = x1
        h2 = x1 * lax.rsqrt(jnp.mean(x1 * x1, axis=-1, keepdims=True) + EPS) * fg_ref[...]
        h_hi = h2.astype(BF16)
        h_lo = (h2 - h_hi.astype(F32)).astype(BF16)
        both = lax.dot_general(wr_ref[...], h_hi, _NT, preferred_element_type=F32)
        cross = lax.dot_general(wr_ref[:ROUTER_ROWS, :], h_lo, _NT, preferred_element_type=F32)
        lt_ref[:, rs] = both[:ROUTER_ROWS] + both[ROUTER_ROWS:] + cross


def _outproj(oa, ob, proj, x2, wd, wn, wo, ffn_gain, wr_t, *, tm):
    T, D = x2.shape
    gate_blk = QKV_W // D
    const = lambda shape: pl.BlockSpec(shape, lambda i: (0, 0), pipeline_mode=pl.Buffered(1))
    return pl.pallas_call(
        functools.partial(_outproj_body, sub=min(tm, 256), ncol=min(D, 1024)),
        grid=(T // tm,),
        in_specs=[pl.BlockSpec((tm, oa.shape[1]), lambda i: (i, 0)),
                  pl.BlockSpec((tm, ob.shape[1]), lambda i: (i, 0)),
                  pl.BlockSpec((tm, D), lambda i: (i, gate_blk)),
                  pl.BlockSpec((tm, D), lambda i: (i, gate_blk + 1)),
                  pl.BlockSpec((tm, D), lambda i: (i, 0)),
                  const(wd.shape), const(wn.shape), const(wo.shape),
                  const((1, D)), const(wr_t.shape)],
        out_specs=[pl.BlockSpec((tm, D), lambda i: (i, 0)),
                   pl.BlockSpec((ROUTER_ROWS, tm), lambda i: (0, i))],
        out_shape=[jax.ShapeDtypeStruct((T, D), F32),
                   jax.ShapeDtypeStruct((ROUTER_ROWS, T), F32)],
        scratch_shapes=[pltpu.VMEM((tm, D), BF16)],
        compiler_params=_cparams(("parallel",)),
        name="outproj",
    )(oa, ob, proj, proj, x2, wd, wn, wo, ffn_gain.reshape(1, D), wr_t)


def _route_body(lt_ref, bias_ref, pos_ref, wts_ref, meta_ref, cnt_scr, carry_scr,
                *, n_groups, n_exp, tm, tc):
    ne = n_groups * n_exp
    phase = pl.program_id(0)
    step = pl.program_id(1)

    l = lt_ref[...] + bias_ref[...]
    el = l[0:ne]
    gl = l[ne:ne + n_groups]
    gmax = jnp.max(gl, axis=0, keepdims=True)
    gi = lax.broadcasted_iota(I32, gl.shape, 0)
    gsel = jnp.min(jnp.where(gl == gmax, gi, n_groups), axis=0, keepdims=True)
    gw = 1.0 / jnp.sum(jnp.exp(gl - gmax), axis=0, keepdims=True)
    ei = lax.broadcasted_iota(I32, el.shape, 0)
    lo = gsel * n_exp
    elm = jnp.where((ei >= lo) & (ei < lo + n_exp), el, -jnp.inf)
    m1 = jnp.max(elm, axis=0, keepdims=True)
    i1 = jnp.min(jnp.where(elm == m1, ei, ne), axis=0, keepdims=True)
    elm2 = jnp.where(ei == i1, -jnp.inf, elm)
    m2 = jnp.max(elm2, axis=0, keepdims=True)
    i2 = jnp.min(jnp.where(elm2 == m2, ei, ne), axis=0, keepdims=True)
    r = jnp.exp(m2 - m1)
    w1 = gw / (1.0 + r)
    w2 = w1 * r
    oh1 = ei == i1
    oh2 = ei == i2
    both = jnp.where(oh1 | oh2, 1.0, 0.0)

    @pl.when((phase == 0) & (step == 0))
    def _():
        cnt_scr[...] = jnp.zeros_like(cnt_scr)
        carry_scr[...] = jnp.zeros_like(carry_scr)

    @pl.when(phase == 0)
    def _():
        cnt_scr[...] += jnp.sum(both, axis=1, keepdims=True)

    @pl.when(phase == 1)
    def _():
        cnt = cnt_scr[...]
        tiles = jnp.floor((cnt + (tm - 1)) * (1.0 / tm))
        ea = lax.broadcasted_iota(I32, (ne, ne), 0)
        eb = lax.broadcasted_iota(I32, (ne, ne), 1)
        lower = jnp.where(eb < ea, 1.0, 0.0).astype(BF16)
        tiles_b = jnp.broadcast_to(tiles, (ne, 128)).astype(BF16)
        start_t = jnp.dot(lower, tiles_b, preferred_element_type=F32)[:, 0:1]
        ta = lax.broadcasted_iota(I32, (tc, tc), 0)
        tb = lax.broadcasted_iota(I32, (tc, tc), 1)
        upper = jnp.where(ta < tb, 1.0, 0.0).astype(BF16)
        prefix = jnp.dot(both.astype(BF16), upper, preferred_element_type=F32)
        base = prefix + carry_scr[...] + start_t * tm
        pos1 = jnp.sum(jnp.where(oh1, base, 0.0), axis=0, keepdims=True)
        pos2 = jnp.sum(jnp.where(oh2, base, 0.0), axis=0, keepdims=True)
        carry_scr[...] += jnp.sum(both, axis=1, keepdims=True)
        pos_ref[...] = jnp.zeros_like(pos_ref)
        pos_ref[0:1, :] = pos1.astype(I32)
        pos_ref[1:2, :] = pos2.astype(I32)
        wts_ref[...] = jnp.zeros_like(wts_ref)
        wts_ref[0:1, :] = w1
        wts_ref[1:2, :] = w2

        nt = meta_ref.shape[1]
        tj = lax.broadcasted_iota(I32, (ne, nt), 1).astype(F32)
        end_t = start_t + tiles
        texp = jnp.sum(jnp.where(end_t <= tj, 1.0, 0.0), axis=0, keepdims=True)
        is_first = jnp.sum(jnp.where((start_t == tj) & (tiles > 0.0), 1.0, 0.0),
                           axis=0, keepdims=True)
        is_last = jnp.sum(jnp.where((end_t - 1.0 == tj) & (tiles > 0.0), 1.0, 0.0),
                          axis=0, keepdims=True)
        nused = jnp.broadcast_to(jnp.sum(tiles, axis=0, keepdims=True), (1, nt))
        meta_ref[...] = jnp.zeros_like(meta_ref)
        meta_ref[0:1, :] = jnp.minimum(texp, ne - 1.0).astype(I32)
        meta_ref[1:2, :] = is_first.astype(I32)
        fill = is_last + jnp.where(tj[0:1, :] >= nused, 1.0, 0.0)
        meta_ref[2:3, :] = fill.astype(I32)
        meta_ref[3:4, :] = nused.astype(I32)
        used = tiles > 0.0
        seg = jnp.sum(jnp.where((end_t <= tj) & used, 1.0, 0.0), axis=0, keepdims=True)
        eidx = lax.broadcasted_iota(I32, (ne, nt), 0).astype(F32)
        nxt = jnp.min(jnp.where((start_t > tj) & used, eidx, float(ne)), axis=0, keepdims=True)
        meta_ref[4:5, :] = (seg - 2.0 * jnp.floor(seg * 0.5)).astype(I32)
        meta_ref[5:6, :] = jnp.where(nxt >= ne, -1.0, nxt).astype(I32)


def _route(lt, bias_col, *, n_groups, n_exp, tm, n_tiles, tc):
    T = lt.shape[1]
    ne = n_groups * n_exp
    nt_pad = -(-n_tiles // 128) * 128
    return pl.pallas_call(
        functools.partial(_route_body, n_groups=n_groups, n_exp=n_exp, tm=tm, tc=tc),
        grid=(2, T // tc),
        in_specs=[pl.BlockSpec((ROUTER_ROWS, tc), lambda p, i: (0, i)),
                  pl.BlockSpec((ROUTER_ROWS, 1), lambda p, i: (0, 0))],
        out_specs=[pl.BlockSpec((8, tc), lambda p, i: (0, i * p)),
                   pl.BlockSpec((8, tc), lambda p, i: (0, i * p)),
                   pl.BlockSpec((8, nt_pad), lambda p, i: (0, 0))],
        out_shape=[jax.ShapeDtypeStruct((8, T), I32),
                   jax.ShapeDtypeStruct((8, T), F32),
                   jax.ShapeDtypeStruct((8, nt_pad), I32)],
        scratch_shapes=[pltpu.VMEM((ne, 1), F32), pltpu.VMEM((ne, 1), F32)],
        compiler_params=_cparams(("arbitrary", "arbitrary")),
        name="route",
    )(lt, bias_col)


DMA_UNROLL = 8


def _dispatch_body(fill_ref, pos1_ref, pos2_ref, h_ref, xs_ref, zero_scr, sem, zsem, *, td, tm):
    n_tiles = fill_ref.shape[0]

    def zero_copy(j):
        return pltpu.make_async_copy(zero_scr, xs_ref.at[pl.ds(pl.multiple_of(j * tm, tm), tm), :], zsem)

    @pl.when(pl.program_id(0) == 0)
    def _():
        zero_scr[...] = jnp.zeros_like(zero_scr)

        def start(j, carry):
            @pl.when(fill_ref[j] == 1)
            def _():
                zero_copy(j).start()
            return carry
        lax.fori_loop(0, n_tiles, start, 0)

        def wait(j, carry):
            @pl.when(fill_ref[j] == 1)
            def _():
                zero_copy(j).wait()
            return carry
        lax.fori_loop(0, n_tiles, wait, 0)

    def row_copy(t, pos):
        return pltpu.make_async_copy(h_ref.at[pl.ds(t, 1), :], xs_ref.at[pl.ds(pos, 1), :], sem)

    def issue(t, carry):
        row_copy(t, pos1_ref[0, 0, t]).start()
        row_copy(t, pos2_ref[0, 0, t]).start()
        return carry
    lax.fori_loop(0, td, issue, 0, unroll=DMA_UNROLL)

    def drain(t, carry):
        row_copy(t, 0).wait()
        row_copy(t, 0).wait()
        return carry
    lax.fori_loop(0, td, drain, 0, unroll=DMA_UNROLL)


def _dispatch(fill, pos1, pos2, rows_in, *, n_rows, td, tm):
    T, W = rows_in.shape
    nblk = T // td
    smem3 = pl.BlockSpec((1, 1, td), lambda i, fl: (i, 0, 0), memory_space=pltpu.SMEM)
    grid_spec = pltpu.PrefetchScalarGridSpec(
        num_scalar_prefetch=1,
        grid=(nblk,),
        in_specs=[smem3, smem3, pl.BlockSpec((td, W), lambda i, fl: (i, 0))],
        out_specs=pl.BlockSpec(memory_space=pl.ANY),
        scratch_shapes=[pltpu.VMEM((tm, W), rows_in.dtype), pltpu.SemaphoreType.DMA(()),
                        pltpu.SemaphoreType.DMA(())],
    )
    return pl.pallas_call(
        functools.partial(_dispatch_body, td=td, tm=tm),
        grid_spec=grid_spec,
        out_shape=jax.ShapeDtypeStruct((n_rows, W), rows_in.dtype),
        compiler_params=_cparams(("arbitrary",)),
        name="dispatch",
    )(fill, pos1.reshape(nblk, 1, td), pos2.reshape(nblk, 1, td), rows_in)


def _expert_weights(plan, j, hbm_refs, buf_refs, bf16_refs, sem):
    texp_ref, first_ref, slot_ref, next_ref = plan

    def copies(e, slot):
        return [pltpu.make_async_copy(h.at[e], b.at[slot], sem.at[i, slot])
                for i, (h, b) in enumerate(zip(hbm_refs, buf_refs))]

    @pl.when(j == 0)
    def _():
        for c in copies(texp_ref[0], 0):
            c.start()

    @pl.when(first_ref[j] == 1)
    def _():
        slot = slot_ref[j]
        for c in copies(texp_ref[j], slot):
            c.wait()

        @pl.when(next_ref[j] >= 0)
        def _():
            for c in copies(next_ref[j], 1 - slot):
                c.start()
        for b, w in zip(buf_refs, bf16_refs):
            w[...] = b[slot].astype(BF16)


def _gateup_body(texp_ref, first_ref, nused_ref, slot_ref, next_ref, x_ref, fg_ref, wg_hbm, wu_hbm,
                 hid_ref, wg_buf, wu_buf, wg_scr, wu_scr, sem):
    j = pl.program_id(0)

    @pl.when(j < nused_ref[0])
    def _():
        _expert_weights((texp_ref, first_ref, slot_ref, next_ref), j, (wg_hbm, wu_hbm),
                        (wg_buf, wu_buf), (wg_scr, wu_scr), sem)
        x = x_ref[...]
        x = (x * lax.rsqrt(jnp.mean(x * x, axis=-1, keepdims=True) + EPS) * fg_ref[...]).astype(BF16)
        g = jnp.dot(x, wg_scr[...], preferred_element_type=F32)
        u = jnp.dot(x, wu_scr[...], preferred_element_type=F32)
        hid_ref[...] = (g * jax.nn.sigmoid(g) * u).astype(BF16)

    @pl.when(j >= nused_ref[0])
    def _():
        hid_ref[...] = jnp.zeros_like(hid_ref)


def _used_tile(j, nused):
    return jnp.minimum(j, nused[0] - 1)


def _gateup(plan, xs, ffn_gain, wg, wu, *, tm):
    P, W = xs.shape
    D, F = wg.shape[1:]
    hbm = pl.BlockSpec(memory_space=pl.ANY)
    grid_spec = pltpu.PrefetchScalarGridSpec(
        num_scalar_prefetch=5,
        grid=(P // tm,),
        in_specs=[pl.BlockSpec((tm, W), lambda j, te, fi, nu, sl, nx: (_used_tile(j, nu), 0)),
                  pl.BlockSpec((1, W), lambda j, te, fi, nu, sl, nx: (0, 0)), hbm, hbm],
        out_specs=pl.BlockSpec((tm, F), lambda j, te, fi, nu, sl, nx: (j, 0)),
        scratch_shapes=[pltpu.VMEM((2, D, F), wg.dtype), pltpu.VMEM((2, D, F), wu.dtype),
                        pltpu.VMEM((D, F), BF16), pltpu.VMEM((D, F), BF16),
                        pltpu.SemaphoreType.DMA((2, 2))],
    )
    return pl.pallas_call(
        _gateup_body,
        grid_spec=grid_spec,
        out_shape=jax.ShapeDtypeStruct((P, F), BF16),
        compiler_params=_cparams(("arbitrary",)),
        name="expert_gateup",
    )(*plan, xs, ffn_gain.reshape(1, W), wg, wu)


def _down_body(texp_ref, first_ref, nused_ref, slot_ref, next_ref, hid_ref, wd_hbm, y_ref,
               wd_buf, wd_scr, sem):
    j = pl.program_id(0)

    @pl.when(j < nused_ref[0])
    def _():
        _expert_weights((texp_ref, first_ref, slot_ref, next_ref), j, (wd_hbm,), (wd_buf,),
                        (wd_scr,), sem)
        y_ref[...] = jnp.dot(hid_ref[...], wd_scr[...], preferred_element_type=F32)

    @pl.when(j >= nused_ref[0])
    def _():
        y_ref[...] = jnp.zeros_like(y_ref)


def _down(plan, hid, wd, *, tm):
    P, F = hid.shape
    D = wd.shape[2]
    grid_spec = pltpu.PrefetchScalarGridSpec(
        num_scalar_prefetch=5,
        grid=(P // tm,),
        in_specs=[pl.BlockSpec((tm, F), lambda j, te, fi, nu, sl, nx: (_used_tile(j, nu), 0)),
                  pl.BlockSpec(memory_space=pl.ANY)],
        out_specs=pl.BlockSpec((tm, D), lambda j, te, fi, nu, sl, nx: (j, 0)),
        scratch_shapes=[pltpu.VMEM((2, F, D), wd.dtype), pltpu.VMEM((F, D), BF16),
                        pltpu.SemaphoreType.DMA((1, 2))],
    )
    return pl.pallas_call(
        _down_body,
        grid_spec=grid_spec,
        out_shape=jax.ShapeDtypeStruct((P, D), F32),
        compiler_params=_cparams(("arbitrary",)),
        name="expert_down",
    )(*plan, hid, wd)


def _combine_body(pos1_ref, pos2_ref, nxt1_ref, nxt2_ref, ys_ref, x1_ref, w1_ref, w2_ref, o_ref,
                  r1_scr, r2_scr, sem, *, tc):
    step = pl.program_id(0)
    slot = step % 2

    def row_copy(pos, dst, s, t):
        return pltpu.make_async_copy(ys_ref.at[pl.ds(pos, 1), :], dst.at[s, pl.ds(t, 1), :], sem.at[s])

    def gather(p1_ref, p2_ref, s):
        def issue(t, carry):
            row_copy(p1_ref[0, 0, t], r1_scr, s, t).start()
            row_copy(p2_ref[0, 0, t], r2_scr, s, t).start()
            return carry
        lax.fori_loop(0, tc, issue, 0, unroll=DMA_UNROLL)

    @pl.when(step == 0)
    def _():
        gather(pos1_ref, pos2_ref, 0)

    @pl.when(step + 1 < pl.num_programs(0))
    def _():
        gather(nxt1_ref, nxt2_ref, 1 - slot)

    def drain(t, carry):
        row_copy(0, r1_scr, slot, t).wait()
        row_copy(0, r2_scr, slot, t).wait()
        return carry
    lax.fori_loop(0, tc, drain, 0, unroll=DMA_UNROLL)

    o_ref[...] = x1_ref[...] + w1_ref[...] * r1_scr[slot] + w2_ref[...] * r2_scr[slot]


def _combine(pos1, pos2, ys, x1, w1, w2, *, tc):
    T, D = x1.shape
    nblk = T // tc
    smem3 = pl.BlockSpec((1, 1, tc), lambda i: (i, 0, 0), memory_space=pltpu.SMEM)
    smem3_next = pl.BlockSpec((1, 1, tc), lambda i: (jnp.minimum(i + 1, nblk - 1), 0, 0),
                              memory_space=pltpu.SMEM)
    col = pl.BlockSpec((tc, 1), lambda i: (i, 0))
    pos1 = pos1.reshape(nblk, 1, tc)
    pos2 = pos2.reshape(nblk, 1, tc)
    return pl.pallas_call(
        functools.partial(_combine_body, tc=tc),
        grid=(nblk,),
        in_specs=[smem3, smem3, smem3_next, smem3_next, pl.BlockSpec(memory_space=pl.ANY),
                  pl.BlockSpec((tc, D), lambda i: (i, 0)), col, col],
        out_specs=pl.BlockSpec((tc, D), lambda i: (i, 0)),
        out_shape=jax.ShapeDtypeStruct((T, D), F32),
        scratch_shapes=[pltpu.VMEM((2, tc, D), F32), pltpu.VMEM((2, tc, D), F32),
                        pltpu.SemaphoreType.DMA((2,))],
        compiler_params=_cparams(("arbitrary",)),
        name="combine",
    )(pos1, pos2, pos1, pos2, ys, x1, w1.reshape(T, 1), w2.reshape(T, 1))


def _tile_sizes(T, D, seq, in_w):
    pick = lambda n, pref: pref if n % pref == 0 else n
    return dict(
        inproj_tm=pick(T, 1024), inproj_tn=next(t for t in (2048, 1024, 512, 256, 128) if in_w % t == 0),
        diff_tq=pick(seq, 256),
        outproj_tm=pick(T, 512),
        route_tc=pick(T, 1024),
        expert_tm=256,
        dispatch_td=pick(T, 1024), combine_tc=pick(T, 256),
    )


def _layer(x2, p, l, *, batch, seq):
    T, D = x2.shape
    ts = _tile_sizes(T, D, seq, p["w_in"].shape[-1])
    lambda_init = 0.8 - 0.6 * math.exp(-0.3 * l)
    n_groups, n_exp = p["b_router_expert"].shape[1:]
    ne = n_groups * n_exp
    rows = seq // GRID_W

    proj = _inproj(x2, p["mix_norm"][l], p["w_in"][l].astype(BF16),
                   tm=ts["inproj_tm"], tn=ts["inproj_tn"])

    lam = p["diff_lambda"][l].astype(F32)
    lam_full = (jnp.exp(jnp.sum(lam[0] * lam[1])) - jnp.exp(jnp.sum(lam[2] * lam[3]))
                + lambda_init).reshape(1)
    slopes = 2.0 ** (-8.0 * jnp.arange(1, DIFF_HEADS + 1, dtype=F32) / DIFF_HEADS)
    two = lambda g: jnp.tile(g.astype(F32), 2).reshape(1, HEAD_W)
    oa = _diff_attention(proj, _alibi_tiles(slopes, seq), lam_full, two(p["diff_q_norm"][l]), two(p["diff_k_norm"][l]),
                         p["diff_subln"][l].astype(F32).reshape(1, HEAD_W),
                         batch=batch, seq=seq, tq=ts["diff_tq"], out_scale=1.0 - lambda_init)

    ob = _na_attention(proj, p["na_q_norm"][l].astype(F32).reshape(1, NA_DIM),
                       p["na_k_norm"][l].astype(F32).reshape(1, NA_DIM),
                       _na_bias_table(p["na_rpb"][l], rows), batch=batch, seq=seq)

    wr_t = jnp.concatenate([p["w_router_expert"][l].T, p["w_router_group"][l].T,
                            jnp.zeros((ROUTER_ROWS - ne - n_groups, D), F32)], axis=0).astype(F32)
    wr_hi = wr_t.astype(BF16)
    wr_t = jnp.concatenate([wr_hi, (wr_t - wr_hi.astype(F32)).astype(BF16)], axis=0)
    b_col = jnp.concatenate([p["b_router_expert"][l].reshape(ne), p["b_router_group"][l],
                             jnp.zeros((ROUTER_ROWS - ne - n_groups,), F32)]).astype(F32).reshape(ROUTER_ROWS, 1)
    x1, lt = _outproj(oa, ob, proj, x2, p["w_diff_out"][l].astype(BF16),
                          p["w_na_out"][l].astype(BF16), p["w_out"][l].astype(BF16),
                          p["ffn_norm"][l].astype(F32), wr_t, tm=ts["outproj_tm"])

    tm = ts["expert_tm"]
    n_tiles = (T * TOP_K) // tm + ne
    pos, wts, meta = _route(lt, b_col, n_groups=n_groups, n_exp=n_exp, tm=tm, n_tiles=n_tiles,
                            tc=ts["route_tc"])
    fill = meta[2, :n_tiles]
    plan = (meta[0, :n_tiles], meta[1, :n_tiles], meta[3, :1], meta[4, :n_tiles], meta[5, :n_tiles])

    xs = _dispatch(fill, pos[0], pos[1], x1, n_rows=n_tiles * tm, td=ts["dispatch_td"], tm=tm)
    F = p["w_expert_gate"].shape[-1]
    hid = _gateup(plan, xs, p["ffn_norm"][l].astype(F32), p["w_expert_gate"][l].reshape(ne, D, F),
                  p["w_expert_up"][l].reshape(ne, D, F), tm=tm)
    ys = _down(plan, hid, p["w_expert_down"][l].reshape(ne, F, D), tm=tm)
    return _combine(pos[0], pos[1], ys, x1, wts[0], wts[1], tc=ts["combine_tc"])


def kernel(x, mix_norm, w_in, diff_q_norm, diff_k_norm, diff_lambda, diff_subln, na_q_norm, na_k_norm, na_rpb, w_diff_out, w_na_out, w_out, ffn_norm, w_router_group, b_router_group, w_router_expert, b_router_expert, w_expert_gate, w_expert_up, w_expert_down):
    B, S, D = x.shape
    assert S % GRID_W == 0 and S % 128 == 0 and QKV_W % D == 0
    p = dict(mix_norm=mix_norm, w_in=w_in, diff_q_norm=diff_q_norm, diff_k_norm=diff_k_norm,
             diff_lambda=diff_lambda, diff_subln=diff_subln, na_q_norm=na_q_norm,
             na_k_norm=na_k_norm, na_rpb=na_rpb, w_diff_out=w_diff_out, w_na_out=w_na_out,
             w_out=w_out, ffn_norm=ffn_norm, w_router_group=w_router_group,
             b_router_group=b_router_group, w_router_expert=w_router_expert,
             b_router_expert=b_router_expert, w_expert_gate=w_expert_gate,
             w_expert_up=w_expert_up, w_expert_down=w_expert_down)
    x2 = x.reshape(B * S, D).astype(F32)
    for l in range(mix_norm.shape[0]):
        x2 = _layer(x2, p, l, batch=B, seq=S)
    return x2.reshape(B, S, D).astype(x.dtype)
```

```python
import functools
import math

import jax
import jax.numpy as jnp
import numpy as np
from jax import lax
from jax.experimental import pallas as pl
from jax.experimental.pallas import tpu as pltpu

F32 = jnp.float32
BF16 = jnp.bfloat16
I32 = jnp.int32

EPS = 1e-6
LOG2E = math.log2(math.e)
GRID_W = 64
DIFF_HEADS = 8
DIFF_QK_DIM = 64
NA_HEADS = 8
NA_DIM = 128
NA_WIN_ROWS = 8
NA_WIN_COLS = 16
TOP_K = 2
HEAD_W = 128
DQ_BLK, DK_BLK, DV_BLK = 0, DIFF_HEADS, 2 * DIFF_HEADS
NQ_BLK = 3 * DIFF_HEADS
NK_BLK = NQ_BLK + NA_HEADS
NV_BLK = NQ_BLK + 2 * NA_HEADS
QKV_W = (3 * DIFF_HEADS + 3 * NA_HEADS) * HEAD_W

ROUTER_ROWS = 48
VMEM_LIMIT = 56 * 1024 * 1024

_NT = (((1,), (1,)), ((), ()))


def _cparams(sem, vmem=VMEM_LIMIT):
    return pltpu.CompilerParams(dimension_semantics=sem, vmem_limit_bytes=vmem)


def _inproj_body(x_ref, g_ref, w_ref, o_ref, h_scr, *, rows):
    @pl.when(pl.program_id(1) == 0)
    def _():
        def chunk(c, carry):
            r = pl.ds(pl.multiple_of(c * rows, rows), rows)
            x = x_ref[r, :]
            ms = jnp.mean(x * x, axis=-1, keepdims=True)
            h_scr[r, :] = (x * lax.rsqrt(ms + EPS) * g_ref[...]).astype(BF16)
            return carry
        lax.fori_loop(0, x_ref.shape[0] // rows, chunk, 0)

    o_ref[...] = jnp.dot(h_scr[...], w_ref[...], preferred_element_type=F32).astype(BF16)


def _inproj(x2, gain, w_bf16, *, tm, tn):
    T, D = x2.shape
    N = w_bf16.shape[1]
    return pl.pallas_call(
        functools.partial(_inproj_body, rows=128),
        grid=(T // tm, N // tn),
        in_specs=[pl.BlockSpec((tm, D), lambda i, j: (i, 0)),
                  pl.BlockSpec((1, D), lambda i, j: (0, 0)),
                  pl.BlockSpec((D, tn), lambda i, j: (0, j))],
        out_specs=pl.BlockSpec((tm, tn), lambda i, j: (i, j)),
        out_shape=jax.ShapeDtypeStruct((T, N), BF16),
        scratch_shapes=[pltpu.VMEM((tm, D), BF16)],
        compiler_params=_cparams(("parallel", "arbitrary")),
        name="inproj",
    )(x2, gain.reshape(1, D), w_bf16)


def _half_rms(x, gain, first):
    x2 = x * x
    s0 = jnp.sum(jnp.where(first, x2, 0.0), axis=-1, keepdims=True)
    s1 = jnp.sum(jnp.where(first, 0.0, x2), axis=-1, keepdims=True)
    inv = jnp.where(first, lax.rsqrt(s0 * (1.0 / DIFF_QK_DIM) + EPS),
                    lax.rsqrt(s1 * (1.0 / DIFF_QK_DIM) + EPS))
    return x * inv * gain


def _alibi_tiles(slopes, seq):
    nb = seq // HEAD_W
    off = np.arange(HEAD_W)
    delta = (np.arange(-(nb - 1), nb)[:, None, None] * HEAD_W + off[None, None, :] - off[None, :, None])
    dist = jnp.asarray(np.abs(delta).astype(np.float32))
    return (-LOG2E * slopes.astype(F32))[:, None, None, None] * dist


def _diff_body(lam_ref, q_ref, k_ref, v_ref, qg_ref, kg_ref, sg_ref, bias_ref, o_ref,
               qm_scr, kn_scr, vt_scr, s0_scr, s1_scr, p_scr, ot_scr, *, tq, seq, out_scale):
    nkb = seq // HEAD_W
    nq = seq // tq
    qblks = tq // HEAD_W
    groups = HEAD_W // 8
    lane = lax.broadcasted_iota(I32, (1, HEAD_W), 1)
    first = lane < DIFF_QK_DIM
    kn_scr[...] = _half_rms(k_ref[...].astype(F32), kg_ref[...], first).astype(BF16)
    vt_scr[:HEAD_W, :] = v_ref[...].astype(F32).T.astype(BF16)
    vt_scr[HEAD_W:, :] = jnp.ones((8, seq), BF16)
    q = _half_rms(q_ref[...].astype(F32), qg_ref[...], first) * (DIFF_QK_DIM ** -0.5 * LOG2E)
    qt = q.T
    top = lax.broadcasted_iota(I32, (HEAD_W, 1), 0) < DIFF_QK_DIM
    for i in range(nq):
        blk = slice(i * tq, (i + 1) * tq)
        qm_scr[0, i] = jnp.where(top, qt[:, blk], 0.0).astype(BF16)
        qm_scr[1, i] = jnp.where(top, 0.0, qt[:, blk]).astype(BF16)
    lam = lam_ref[0]

    def stage(score, expo):
        mx = [jnp.full((8, tq), -jnp.inf, F32), jnp.full((8, tq), -jnp.inf, F32)]
        if score is not None:
            qa, dst = score
            for m in range(2):
                s_all = jnp.dot(kn_scr[...], qm_scr[m, qa], preferred_element_type=F32)
                for c in range(nkb):
                    ks = slice(c * HEAD_W, (c + 1) * HEAD_W)
                    tile0 = qa * qblks - c + (nkb - 1)
                    bias = jnp.concatenate([bias_ref[0, tile0 + t] for t in range(qblks)], axis=1)
                    s = s_all[ks, :] + bias
                    dst[m, ks, :] = s.astype(BF16)
                    mx[m] = jnp.maximum(mx[m], jnp.max(s.reshape(groups, 8, tq), axis=0))
        for c in range(nkb):
            ks = slice(c * HEAD_W, (c + 1) * HEAD_W)
            if expo is not None:
                src, mrow, buf = expo
                for m in range(2):
                    p_scr[buf, m, ks, :] = jnp.exp2(src[m, ks, :] - mrow[m].astype(BF16))
        return tuple(jnp.max(a, axis=0, keepdims=True) for a in mx)

    def values(qb, buf):
        outs = []
        for m in range(2):
            ol = jnp.dot(vt_scr[...], p_scr[buf, m], preferred_element_type=F32)
            outs.append(ol[:HEAD_W] / ol[HEAD_W:HEAD_W + 1])
        ot_scr[qb] = outs[0] - lam * outs[1]

    mrow = stage((0, s0_scr), None)

    def pair(j, mrow):
        mrow1 = stage((2 * j + 1, s1_scr), (s0_scr, mrow, 0))
        values(2 * j, 0)
        mrow0 = stage((2 * j + 2, s0_scr), (s1_scr, mrow1, 1))
        values(2 * j + 1, 1)
        return mrow0
    mrow = lax.fori_loop(0, nq // 2 - 1, pair, mrow)
    mrow1 = stage((nq - 1, s1_scr), (s0_scr, mrow, 0))
    values(nq - 2, 0)
    stage(None, (s1_scr, mrow1, 1))
    values(nq - 1, 1)

    gain = sg_ref[...] * out_scale
    for i in range(nq):
        o = ot_scr[i]
        o = o * lax.rsqrt(jnp.mean(o * o, axis=0, keepdims=True) + EPS)
        o_ref[i * tq:(i + 1) * tq, :] = (o.T * gain).astype(BF16)


def _diff_attention(proj, bias_tiles, lam, q_gain, k_gain, sub_gain, *, batch, seq, tq, out_scale):
    T = proj.shape[0]
    smem = pl.BlockSpec(memory_space=pltpu.SMEM)
    vec = pl.BlockSpec((1, HEAD_W), lambda h, b: (0, 0))
    blk = lambda col: pl.BlockSpec((seq, HEAD_W), lambda h, b: (b, col + h))
    return pl.pallas_call(
        functools.partial(_diff_body, tq=tq, seq=seq, out_scale=out_scale),
        grid=(DIFF_HEADS, batch),
        in_specs=[smem, blk(DQ_BLK), blk(DK_BLK), blk(DV_BLK), vec, vec, vec,
                  pl.BlockSpec((1,) + bias_tiles.shape[1:], lambda h, b: (h, 0, 0, 0))],
        out_specs=pl.BlockSpec((seq, HEAD_W), lambda h, b: (b, h)),
        out_shape=jax.ShapeDtypeStruct((T, DIFF_HEADS * HEAD_W), BF16),
        scratch_shapes=[pltpu.VMEM((2, seq // tq, HEAD_W, tq), BF16), pltpu.VMEM((seq, HEAD_W), BF16),
                        pltpu.VMEM((HEAD_W + 8, seq), BF16),
                        pltpu.VMEM((2, seq, tq), BF16), pltpu.VMEM((2, seq, tq), BF16),
                        pltpu.VMEM((2, 2, seq, tq), BF16), pltpu.VMEM((seq // tq, HEAD_W, tq), F32)],
        compiler_params=_cparams(("parallel", "parallel")),
        name="diff_attn",
    )(lam, proj, proj, proj, q_gain, k_gain, sub_gain, bias_tiles)


NA_GROUP_ROWS = 4


def _na_plan(rows):
    wr = min(NA_WIN_ROWS, rows)
    gs = NA_GROUP_ROWS if rows % NA_GROUP_ROWS == 0 else 1
    ww = wr + gs - 1
    ww = min(ww + ww % 2, rows)
    row_start = lambda r: min(max(r - wr // 2, 0), rows - wr)
    groups, patterns = [], {}
    for g in range(rows // gs):
        ws = min(row_start(g * gs), rows - ww)
        key = tuple((row_start(r) - ws, r - ws) for r in range(g * gs, (g + 1) * gs))
        groups.append((ws, patterns.setdefault(key, len(patterns))))
    return wr, gs, ww, groups, list(patterns)


def _na_bias_table(rpb, rows):
    wr, gs, ww, _, patterns = _na_plan(rows)
    H = rpb.shape[0]
    c = np.arange(GRID_W)
    col_start = np.clip(c - NA_WIN_COLS // 2, 0, GRID_W - NA_WIN_COLS)
    col_in = (c[None, :] >= col_start[:, None]) & (c[None, :] < col_start[:, None] + NA_WIN_COLS)
    dc = np.clip(c[None, :] - c[:, None], -(NA_WIN_COLS - 1), NA_WIN_COLS - 1) + (NA_WIN_COLS - 1)
    sel = jnp.asarray((dc[:, :, None] == np.arange(2 * NA_WIN_COLS - 1)[None, None, :]).astype(np.float32))
    scaled = rpb.astype(F32) * LOG2E
    pats = []
    for key in patterns:
        blocks = []
        for win_off, row_off in key:
            lo = win_off - row_off + NA_WIN_ROWS - 1
            vals = jnp.einsum("hwd,ckd->hcwk", scaled[:, lo:lo + wr, :], sel,
                              precision=lax.Precision.HIGHEST)
            vals = jnp.where(jnp.asarray(col_in)[None, :, None, :], vals, -jnp.inf)
            pad = lambda n: jnp.full((H, GRID_W, n, GRID_W), -jnp.inf, F32)
            full = jnp.concatenate([pad(win_off), vals, pad(ww - win_off - wr)], axis=2)
            blocks.append(full.reshape(H, GRID_W, ww * GRID_W))
        pats.append(jnp.concatenate(blocks, axis=1))
    return jnp.stack(pats, axis=1)


def _na_body(q_ref, k_ref, v_ref, qg_ref, kg_ref, bias_ref, o_ref, qn_scr, kn_scr, va_scr, *, rows):
    wr, gs, ww, groups, _ = _na_plan(rows)

    def rms(x, g):
        return x * lax.rsqrt(jnp.mean(x * x, axis=-1, keepdims=True) + EPS) * g

    qn_scr[...] = (rms(q_ref[...].astype(F32), qg_ref[...]) * (NA_DIM ** -0.5 * LOG2E)).astype(BF16)
    kn_scr[...] = rms(k_ref[...].astype(F32), kg_ref[...]).astype(BF16)
    va_scr[:, :NA_DIM] = v_ref[...]
    va_scr[:, NA_DIM:] = jnp.ones((v_ref.shape[0], NA_DIM), BF16)

    for g, (ws, pid) in enumerate(groups):
        qs = slice(g * gs * GRID_W, (g + 1) * gs * GRID_W)
        ks = slice(ws * GRID_W, (ws + ww) * GRID_W)
        s = lax.dot_general(qn_scr[qs, :], kn_scr[ks, :], _NT, preferred_element_type=F32)
        s = s + bias_ref[0, pid]
        m = jnp.max(s, axis=-1, keepdims=True)
        p = jnp.exp2((s - m).astype(BF16))
        ol = jnp.dot(p, va_scr[ks, :], preferred_element_type=F32)
        o_ref[qs, :] = (ol[:, :NA_DIM] / ol[:, NA_DIM:]).astype(BF16)


def _na_attention(proj, q_gain, k_gain, bias_tab, *, batch, seq):
    T = proj.shape[0]
    rows = seq // GRID_W
    vec = pl.BlockSpec((1, NA_DIM), lambda h, b: (0, 0))
    return pl.pallas_call(
        functools.partial(_na_body, rows=rows),
        grid=(NA_HEADS, batch),
        in_specs=[pl.BlockSpec((seq, HEAD_W), lambda h, b: (b, NQ_BLK + h)),
                  pl.BlockSpec((seq, HEAD_W), lambda h, b: (b, NK_BLK + h)),
                  pl.BlockSpec((seq, HEAD_W), lambda h, b: (b, NV_BLK + h)),
                  vec, vec,
                  pl.BlockSpec((1,) + bias_tab.shape[1:], lambda h, b: (h, 0, 0, 0))],
        out_specs=pl.BlockSpec((seq, HEAD_W), lambda h, b: (b, h)),
        out_shape=jax.ShapeDtypeStruct((T, NA_HEADS * NA_DIM), BF16),
        scratch_shapes=[pltpu.VMEM((seq, HEAD_W), BF16), pltpu.VMEM((seq, HEAD_W), BF16),
                        pltpu.VMEM((seq, 2 * NA_DIM), BF16)],
        compiler_params=_cparams(("parallel", "parallel")),
        name="na_attn",
    )(proj, proj, proj, q_gain, k_gain, bias_tab)


def _outproj_body(oa_ref, ob_ref, ga_ref, gb_ref, x_ref, wd_ref, wn_ref, wo_ref, fg_ref, wr_ref,
                  x1_ref, lt_ref, m_scr, *, sub, ncol):
    tm, D = x_ref.shape
    for r in range(tm // sub):
        rs = slice(r * sub, (r + 1) * sub)
        oa = oa_ref[rs, :]
        ob = ob_ref[rs, :]
        for c in range(D // ncol):
            cs = slice(c * ncol, (c + 1) * ncol)
            ya = jnp.dot(oa, wd_ref[:, cs], preferred_element_type=F32)
            yb = jnp.dot(ob, wn_ref[:, cs], preferred_element_type=F32)
            m_scr[rs, cs] = (jax.nn.sigmoid(ga_ref[rs, cs].astype(F32)) * ya
                             + jax.nn.sigmoid(gb_ref[rs, cs].astype(F32)) * yb).astype(BF16)
        x1 = x_ref[rs, :] + jnp.dot(m_scr[rs, :], wo_ref[...], preferred_element_type=F32)
        x1_ref[rs, :] = x1
        h2 = x1 * lax.rsqrt(jnp.mean(x1 * x1, axis=-1, keepdims=True) + EPS) * fg_ref[...]
        h_hi = h2.astype(BF16)
        h_lo = (h2 - h_hi.astype(F32)).astype(BF16)
        both = lax.dot_general(wr_ref[...], h_hi, _NT, preferred_element_type=F32)
        cross = lax.dot_general(wr_ref[:ROUTER_ROWS, :], h_lo, _NT, preferred_element_type=F32)
        lt_ref[:, rs] = both[:ROUTER_ROWS] + both[ROUTER_ROWS:] + cross


def _outproj(oa, ob, proj, x2, wd, wn, wo, ffn_gain, wr_t, *, tm):
    T, D = x2.shape
    gate_blk = QKV_W // D
    const = lambda shape: pl.BlockSpec(shape, lambda i: (0, 0), pipeline_mode=pl.Buffered(1))
    return pl.pallas_call(
        functools.partial(_outproj_body, sub=min(tm, 256), ncol=min(D, 1024)),
        grid=(T // tm,),
        in_specs=[pl.BlockSpec((tm, oa.shape[1]), lambda i: (i, 0)),
                  pl.BlockSpec((tm, ob.shape[1]), lambda i: (i, 0)),
                  pl.BlockSpec((tm, D), lambda i: (i, gate_blk)),
                  pl.BlockSpec((tm, D), lambda i: (i, gate_blk + 1)),
                  pl.BlockSpec((tm, D), lambda i: (i, 0)),
                  const(wd.shape), const(wn.shape), const(wo.shape),
                  const((1, D)), const(wr_t.shape)],
        out_specs=[pl.BlockSpec((tm, D), lambda i: (i, 0)),
                   pl.BlockSpec((ROUTER_ROWS, tm), lambda i: (0, i))],
        out_shape=[jax.ShapeDtypeStruct((T, D), F32),
                   jax.ShapeDtypeStruct((ROUTER_ROWS, T), F32)],
        scratch_shapes=[pltpu.VMEM((tm, D), BF16)],
        compiler_params=_cparams(("parallel",)),
        name="outproj",
    )(oa, ob, proj, proj, x2, wd, wn, wo, ffn_gain.reshape(1, D), wr_t)


def _route_body(lt_ref, bias_ref, pos_ref, wts_ref, meta_ref, cnt_scr, carry_scr,
                *, n_groups, n_exp, tm, tc):
    ne = n_groups * n_exp
    phase = pl.program_id(0)
    step = pl.program_id(1)

    l = lt_ref[...] + bias_ref[...]
    el = l[0:ne]
    gl = l[ne:ne + n_groups]
    gmax = jnp.max(gl, axis=0, keepdims=True)
    gi = lax.broadcasted_iota(I32, gl.shape, 0)
    gsel = jnp.min(jnp.where(gl == gmax, gi, n_groups), axis=0, keepdims=True)
    gw = 1.0 / jnp.sum(jnp.exp(gl - gmax), axis=0, keepdims=True)
    ei = lax.broadcasted_iota(I32, el.shape, 0)
    lo = gsel * n_exp
    elm = jnp.where((ei >= lo) & (ei < lo + n_exp), el, -jnp.inf)
    m1 = jnp.max(elm, axis=0, keepdims=True)
    i1 = jnp.min(jnp.where(elm == m1, ei, ne), axis=0, keepdims=True)
    elm2 = jnp.where(ei == i1, -jnp.inf, elm)
    m2 = jnp.max(elm2, axis=0, keepdims=True)
    i2 = jnp.min(jnp.where(elm2 == m2, ei, ne), axis=0, keepdims=True)
    r = jnp.exp(m2 - m1)
    w1 = gw / (1.0 + r)
    w2 = w1 * r
    oh1 = ei == i1
    oh2 = ei == i2
    both = jnp.where(oh1 | oh2, 1.0, 0.0)

    @pl.when((phase == 0) & (step == 0))
    def _():
        cnt_scr[...] = jnp.zeros_like(cnt_scr)
        carry_scr[...] = jnp.zeros_like(carry_scr)

    @pl.when(phase == 0)
    def _():
        cnt_scr[...] += jnp.sum(both, axis=1, keepdims=True)

    @pl.when(phase == 1)
    def _():
        cnt = cnt_scr[...]
        tiles = jnp.floor((cnt + (tm - 1)) * (1.0 / tm))
        ea = lax.broadcasted_iota(I32, (ne, ne), 0)
        eb = lax.broadcasted_iota(I32, (ne, ne), 1)
        lower = jnp.where(eb < ea, 1.0, 0.0).astype(BF16)
        tiles_b = jnp.broadcast_to(tiles, (ne, 128)).astype(BF16)
        start_t = jnp.dot(lower, tiles_b, preferred_element_type=F32)[:, 0:1]
        ta = lax.broadcasted_iota(I32, (tc, tc), 0)
        tb = lax.broadcasted_iota(I32, (tc, tc), 1)
        upper = jnp.where(ta < tb, 1.0, 0.0).astype(BF16)
        prefix = jnp.dot(both.astype(BF16), upper, preferred_element_type=F32)
        base = prefix + carry_scr[...] + start_t * tm
        pos1 = jnp.sum(jnp.where(oh1, base, 0.0), axis=0, keepdims=True)
        pos2 = jnp.sum(jnp.where(oh2, base, 0.0), axis=0, keepdims=True)
        carry_scr[...] += jnp.sum(both, axis=1, keepdims=True)
        pos_ref[...] = jnp.zeros_like(pos_ref)
        pos_ref[0:1, :] = pos1.astype(I32)
        pos_ref[1:2, :] = pos2.astype(I32)
        wts_ref[...] = jnp.zeros_like(wts_ref)
        wts_ref[0:1, :] = w1
        wts_ref[1:2, :] = w2

        nt = meta_ref.shape[1]
        tj = lax.broadcasted_iota(I32, (ne, nt), 1).astype(F32)
        end_t = start_t + tiles
        texp = jnp.sum(jnp.where(end_t <= tj, 1.0, 0.0), axis=0, keepdims=True)
        is_first = jnp.sum(jnp.where((start_t == tj) & (tiles > 0.0), 1.0, 0.0),
                           axis=0, keepdims=True)
        is_last = jnp.sum(jnp.where((end_t - 1.0 == tj) & (tiles > 0.0), 1.0, 0.0),
                          axis=0, keepdims=True)
        nused = jnp.broadcast_to(jnp.sum(tiles, axis=0, keepdims=True), (1, nt))
        meta_ref[...] = jnp.zeros_like(meta_ref)
        meta_ref[0:1, :] = jnp.minimum(texp, ne - 1.0).astype(I32)
        meta_ref[1:2, :] = is_first.astype(I32)
        fill = is_last + jnp.where(tj[0:1, :] >= nused, 1.0, 0.0)
        meta_ref[2:3, :] = fill.astype(I32)
        meta_ref[3:4, :] = nused.astype(I32)
        used = tiles > 0.0
        seg = jnp.sum(jnp.where((end_t <= tj) & used, 1.0, 0.0), axis=0, keepdims=True)
        eidx = lax.broadcasted_iota(I32, (ne, nt), 0).astype(F32)
        nxt = jnp.min(jnp.where((start_t > tj) & used, eidx, float(ne)), axis=0, keepdims=True)
        meta_ref[4:5, :] = (seg - 2.0 * jnp.floor(seg * 0.5)).astype(I32)
        meta_ref[5:6, :] = jnp.where(nxt >= ne, -1.0, nxt).astype(I32)


def _route(lt, bias_col, *, n_groups, n_exp, tm, n_tiles, tc):
    T = lt.shape[1]
    ne = n_groups * n_exp
    nt_pad = -(-n_tiles // 128) * 128
    return pl.pallas_call(
        functools.partial(_route_body, n_groups=n_groups, n_exp=n_exp, tm=tm, tc=tc),
        grid=(2, T // tc),
        in_specs=[pl.BlockSpec((ROUTER_ROWS, tc), lambda p, i: (0, i)),
                  pl.BlockSpec((ROUTER_ROWS, 1), lambda p, i: (0, 0))],
        out_specs=[pl.BlockSpec((8, tc), lambda p, i: (0, i * p)),
                   pl.BlockSpec((8, tc), lambda p, i: (0, i * p)),
                   pl.BlockSpec((8, nt_pad), lambda p, i: (0, 0))],
        out_shape=[jax.ShapeDtypeStruct((8, T), I32),
                   jax.ShapeDtypeStruct((8, T), F32),
                   jax.ShapeDtypeStruct((8, nt_pad), I32)],
        scratch_shapes=[pltpu.VMEM((ne, 1), F32), pltpu.VMEM((ne, 1), F32)],
        compiler_params=_cparams(("arbitrary", "arbitrary")),
        name="route",
    )(lt, bias_col)


DMA_UNROLL = 8


def _dispatch_body(fill_ref, pos1_ref, pos2_ref, h_ref, xs_ref, zero_scr, sem, zsem, *, td, tm):
    n_tiles = fill_ref.shape[0]

    def zero_copy(j):
        return pltpu.make_async_copy(zero_scr, xs_ref.at[pl.ds(pl.multiple_of(j * tm, tm), tm), :], zsem)

    @pl.when(pl.program_id(0) == 0)
    def _():
        zero_scr[...] = jnp.zeros_like(zero_scr)

        def start(j, carry):
            @pl.when(fill_ref[j] == 1)
            def _():
                zero_copy(j).start()
            return carry
        lax.fori_loop(0, n_tiles, start, 0)

        def wait(j, carry):
            @pl.when(fill_ref[j] == 1)
            def _():
                zero_copy(j).wait()
            return carry
        lax.fori_loop(0, n_tiles, wait, 0)

    def row_copy(t, pos):
        return pltpu.make_async_copy(h_ref.at[pl.ds(t, 1), :], xs_ref.at[pl.ds(pos, 1), :], sem)

    def issue(t, carry):
        row_copy(t, pos1_ref[0, 0, t]).start()
        row_copy(t, pos2_ref[0, 0, t]).start()
        return carry
    lax.fori_loop(0, td, issue, 0, unroll=DMA_UNROLL)

    def drain(t, carry):
        row_copy(t, 0).wait()
        row_copy(t, 0).wait()
        return carry
    lax.fori_loop(0, td, drain, 0, unroll=DMA_UNROLL)


def _dispatch(fill, pos1, pos2, rows_in, *, n_rows, td, tm):
    T, W = rows_in.shape
    nblk = T // td
    smem3 = pl.BlockSpec((1, 1, td), lambda i, fl: (i, 0, 0), memory_space=pltpu.SMEM)
    grid_spec = pltpu.PrefetchScalarGridSpec(
        num_scalar_prefetch=1,
        grid=(nblk,),
        in_specs=[smem3, smem3, pl.BlockSpec((td, W), lambda i, fl: (i, 0))],
        out_specs=pl.BlockSpec(memory_space=pl.ANY),
        scratch_shapes=[pltpu.VMEM((tm, W), rows_in.dtype), pltpu.SemaphoreType.DMA(()),
                        pltpu.SemaphoreType.DMA(())],
    )
    return pl.pallas_call(
        functools.partial(_dispatch_body, td=td, tm=tm),
        grid_spec=grid_spec,
        out_shape=jax.ShapeDtypeStruct((n_rows, W), rows_in.dtype),
        compiler_params=_cparams(("arbitrary",)),
        name="dispatch",
    )(fill, pos1.reshape(nblk, 1, td), pos2.reshape(nblk, 1, td), rows_in)


def _expert_weights(plan, j, hbm_refs, buf_refs, bf16_refs, sem):
    texp_ref, first_ref, slot_ref, next_ref = plan

    def copies(e, slot):
        return [pltpu.make_async_copy(h.at[e], b.at[slot], sem.at[i, slot])
                for i, (h, b) in enumerate(zip(hbm_refs, buf_refs))]

    @pl.when(j == 0)
    def _():
        for c in copies(texp_ref[0], 0):
            c.start()

    @pl.when(first_ref[j] == 1)
    def _():
        slot = slot_ref[j]
        for c in copies(texp_ref[j], slot):
            c.wait()

        @pl.when(next_ref[j] >= 0)
        def _():
            for c in copies(next_ref[j], 1 - slot):
                c.start()
        for b, w in zip(buf_refs, bf16_refs):
            w[...] = b[slot].astype(BF16)


def _gateup_body(texp_ref, first_ref, nused_ref, slot_ref, next_ref, x_ref, fg_ref, wg_hbm, wu_hbm,
                 hid_ref, wg_buf, wu_buf, wg_scr, wu_scr, sem):
    j = pl.program_id(0)

    @pl.when(j < nused_ref[0])
    def _():
        _expert_weights((texp_ref, first_ref, slot_ref, next_ref), j, (wg_hbm, wu_hbm),
                        (wg_buf, wu_buf), (wg_scr, wu_scr), sem)
        x = x_ref[...]
        x = (x * lax.rsqrt(jnp.mean(x * x, axis=-1, keepdims=True) + EPS) * fg_ref[...]).astype(BF16)
        g = jnp.dot(x, wg_scr[...], preferred_element_type=F32)
        u = jnp.dot(x, wu_scr[...], preferred_element_type=F32)
        hid_ref[...] = (g * jax.nn.sigmoid(g) * u).astype(BF16)

    @pl.when(j >= nused_ref[0])
    def _():
        hid_ref[...] = jnp.zeros_like(hid_ref)


def _used_tile(j, nused):
    return jnp.minimum(j, nused[0] - 1)


def _gateup(plan, xs, ffn_gain, wg, wu, *, tm):
    P, W = xs.shape
    D, F = wg.shape[1:]
    hbm = pl.BlockSpec(memory_space=pl.ANY)
    grid_spec = pltpu.PrefetchScalarGridSpec(
        num_scalar_prefetch=5,
        grid=(P // tm,),
        in_specs=[pl.BlockSpec((tm, W), lambda j, te, fi, nu, sl, nx: (_used_tile(j, nu), 0)),
                  pl.BlockSpec((1, W), lambda j, te, fi, nu, sl, nx: (0, 0)), hbm, hbm],
        out_specs=pl.BlockSpec((tm, F), lambda j, te, fi, nu, sl, nx: (j, 0)),
        scratch_shapes=[pltpu.VMEM((2, D, F), wg.dtype), pltpu.VMEM((2, D, F), wu.dtype),
                        pltpu.VMEM((D, F), BF16), pltpu.VMEM((D, F), BF16),
                        pltpu.SemaphoreType.DMA((2, 2))],
    )
    return pl.pallas_call(
        _gateup_body,
        grid_spec=grid_spec,
        out_shape=jax.ShapeDtypeStruct((P, F), BF16),
        compiler_params=_cparams(("arbitrary",)),
        name="expert_gateup",
    )(*plan, xs, ffn_gain.reshape(1, W), wg, wu)


def _down_body(texp_ref, first_ref, nused_ref, slot_ref, next_ref, hid_ref, wd_hbm, y_ref,
               wd_buf, wd_scr, sem):
    j = pl.program_id(0)

    @pl.when(j < nused_ref[0])
    def _():
        _expert_weights((texp_ref, first_ref, slot_ref, next_ref), j, (wd_hbm,), (wd_buf,),
                        (wd_scr,), sem)
        y_ref[...] = jnp.dot(hid_ref[...], wd_scr[...], preferred_element_type=F32)

    @pl.when(j >= nused_ref[0])
    def _():
        y_ref[...] = jnp.zeros_like(y_ref)


def _down(plan, hid, wd, *, tm):
    P, F = hid.shape
    D = wd.shape[2]
    grid_spec = pltpu.PrefetchScalarGridSpec(
        num_scalar_prefetch=5,
        grid=(P // tm,),
        in_specs=[pl.BlockSpec((tm, F), lambda j, te, fi, nu, sl, nx: (_used_tile(j, nu), 0)),
                  pl.BlockSpec(memory_space=pl.ANY)],
        out_specs=pl.BlockSpec((tm, D), lambda j, te, fi, nu, sl, nx: (j, 0)),
        scratch_shapes=[pltpu.VMEM((2, F, D), wd.dtype), pltpu.VMEM((F, D), BF16),
                        pltpu.SemaphoreType.DMA((1, 2))],
    )
    return pl.pallas_call(
        _down_body,
        grid_spec=grid_spec,
        out_shape=jax.ShapeDtypeStruct((P, D), F32),
        compiler_params=_cparams(("arbitrary",)),
        name="expert_down",
    )(*plan, hid, wd)


def _combine_body(pos1_ref, pos2_ref, nxt1_ref, nxt2_ref, ys_ref, x1_ref, w1_ref, w2_ref, o_ref,
                  r1_scr, r2_scr, sem, *, tc):
    step = pl.program_id(0)
    slot = step % 2

    def row_copy(pos, dst, s, t):
        return pltpu.make_async_copy(ys_ref.at[pl.ds(pos, 1), :], dst.at[s, pl.ds(t, 1), :], sem.at[s])

    def gather(p1_ref, p2_ref, s):
        def issue(t, carry):
            row_copy(p1_ref[0, 0, t], r1_scr, s, t).start()
            row_copy(p2_ref[0, 0, t], r2_scr, s, t).start()
            return carry
        lax.fori_loop(0, tc, issue, 0, unroll=DMA_UNROLL)

    @pl.when(step == 0)
    def _():
        gather(pos1_ref, pos2_ref, 0)

    @pl.when(step + 1 < pl.num_programs(0))
    def _():
        gather(nxt1_ref, nxt2_ref, 1 - slot)

    def drain(t, carry):
        row_copy(0, r1_scr, slot, t).wait()
        row_copy(0, r2_scr, slot, t).wait()
        return carry
    lax.fori_loop(0, tc, drain, 0, unroll=DMA_UNROLL)

    o_ref[...] = x1_ref[...] + w1_ref[...] * r1_scr[slot] + w2_ref[...] * r2_scr[slot]


def _combine(pos1, pos2, ys, x1, w1, w2, *, tc):
    T, D = x1.shape
    nblk = T // tc
    smem3 = pl.BlockSpec((1, 1, tc), lambda i: (i, 0, 0), memory_space=pltpu.SMEM)
    smem3_next = pl.BlockSpec((1, 1, tc), lambda i: (jnp.minimum(i + 1, nblk - 1), 0, 0),
                              memory_space=pltpu.SMEM)
    col = pl.BlockSpec((tc, 1), lambda i: (i, 0))
    pos1 = pos1.reshape(nblk, 1, tc)
    pos2 = pos2.reshape(nblk, 1, tc)
    return pl.pallas_call(
        functools.partial(_combine_body, tc=tc),
        grid=(nblk,),
        in_specs=[smem3, smem3, smem3_next, smem3_next, pl.BlockSpec(memory_space=pl.ANY),
                  pl.BlockSpec((tc, D), lambda i: (i, 0)), col, col],
        out_specs=pl.BlockSpec((tc, D), lambda i: (i, 0)),
        out_shape=jax.ShapeDtypeStruct((T, D), F32),
        scratch_shapes=[pltpu.VMEM((2, tc, D), F32), pltpu.VMEM((2, tc, D), F32),
                        pltpu.SemaphoreType.DMA((2,))],
        compiler_params=_cparams(("arbitrary",)),
        name="combine",
    )(pos1, pos2, pos1, pos2, ys, x1, w1.reshape(T, 1), w2.reshape(T, 1))


def _tile_sizes(T, D, seq, in_w):
    pick = lambda n, pref: pref if n % pref == 0 else n
    return dict(
        inproj_tm=pick(T, 1024), inproj_tn=next(t for t in (2048, 1024, 512, 256, 128) if in_w % t == 0),
        diff_tq=pick(seq, 256),
        outproj_tm=pick(T, 512),
        route_tc=pick(T, 1024),
        expert_tm=256,
        dispatch_td=pick(T, 1024), combine_tc=pick(T, 256),
    )


def _layer(x2, p, l, *, batch, seq):
    T, D = x2.shape
    ts = _tile_sizes(T, D, seq, p["w_in"].shape[-1])
    lambda_init = 0.8 - 0.6 * math.exp(-0.3 * l)
    n_groups, n_exp = p["b_router_expert"].shape[1:]
    ne = n_groups * n_exp
    rows = seq // GRID_W

    proj = _inproj(x2, p["mix_norm"][l], p["w_in"][l].astype(BF16),
                   tm=ts["inproj_tm"], tn=ts["inproj_tn"])

    lam = p["diff_lambda"][l].astype(F32)
    lam_full = (jnp.exp(jnp.sum(lam[0] * lam[1])) - jnp.exp(jnp.sum(lam[2] * lam[3]))
                + lambda_init).reshape(1)
    slopes = 2.0 ** (-8.0 * jnp.arange(1, DIFF_HEADS + 1, dtype=F32) / DIFF_HEADS)
    two = lambda g: jnp.tile(g.astype(F32), 2).reshape(1, HEAD_W)
    oa = _diff_attention(proj, _alibi_tiles(slopes, seq), lam_full, two(p["diff_q_norm"][l]), two(p["diff_k_norm"][l]),
                         p["diff_subln"][l].astype(F32).reshape(1, HEAD_W),
                         batch=batch, seq=seq, tq=ts["diff_tq"], out_scale=1.0 - lambda_init)

    ob = _na_attention(proj, p["na_q_norm"][l].astype(F32).reshape(1, NA_DIM),
                       p["na_k_norm"][l].astype(F32).reshape(1, NA_DIM),
                       _na_bias_table(p["na_rpb"][l], rows), batch=batch, seq=seq)

    wr_t = jnp.concatenate([p["w_router_expert"][l].T, p["w_router_group"][l].T,
                            jnp.zeros((ROUTER_ROWS - ne - n_groups, D), F32)], axis=0).astype(F32)
    wr_hi = wr_t.astype(BF16)
    wr_t = jnp.concatenate([wr_hi, (wr_t - wr_hi.astype(F32)).astype(BF16)], axis=0)
    b_col = jnp.concatenate([p["b_router_expert"][l].reshape(ne), p["b_router_group"][l],
                             jnp.zeros((ROUTER_ROWS - ne - n_groups,), F32)]).astype(F32).reshape(ROUTER_ROWS, 1)
    x1, lt = _outproj(oa, ob, proj, x2, p["w_diff_out"][l].astype(BF16),
                          p["w_na_out"][l].astype(BF16), p["w_out"][l].astype(BF16),
                          p["ffn_norm"][l].astype(F32), wr_t, tm=ts["outproj_tm"])

    tm = ts["expert_tm"]
    n_tiles = (T * TOP_K) // tm + ne
    pos, wts, meta = _route(lt, b_col, n_groups=n_groups, n_exp=n_exp, tm=tm, n_tiles=n_tiles,
                            tc=ts["route_tc"])
    fill = meta[2, :n_tiles]
    plan = (meta[0, :n_tiles], meta[1, :n_tiles], meta[3, :1], meta[4, :n_tiles], meta[5, :n_tiles])

    xs = _dispatch(fill, pos[0], pos[1], x1, n_rows=n_tiles * tm, td=ts["dispatch_td"], tm=tm)
    F = p["w_expert_gate"].shape[-1]
    hid = _gateup(plan, xs, p["ffn_norm"][l].astype(F32), p["w_expert_gate"][l].reshape(ne, D, F),
                  p["w_expert_up"][l].reshape(ne, D, F), tm=tm)
    ys = _down(plan, hid, p["w_expert_down"][l].reshape(ne, F, D), tm=tm)
    return _combine(pos[0], pos[1], ys, x1, wts[0], wts[1], tc=ts["combine_tc"])


def kernel(x, mix_norm, w_in, diff_q_norm, diff_k_norm, diff_lambda, diff_subln, na_q_norm, na_k_norm, na_rpb, w_diff_out, w_na_out, w_out, ffn_norm, w_router_group, b_router_group, w_router_expert, b_router_expert, w_expert_gate, w_expert_up, w_expert_down):
    B, S, D = x.shape
    assert S % GRID_W == 0 and S % 128 == 0 and QKV_W % D == 0
    p = dict(mix_norm=mix_norm, w_in=w_in, diff_q_norm=diff_q_norm, diff_k_norm=diff_k_norm,
             diff_lambda=diff_lambda, diff_subln=diff_subln, na_q_norm=na_q_norm,
             na_k_norm=na_k_norm, na_rpb=na_rpb, w_diff_out=w_diff_out, w_na_out=w_na_out,
             w_out=w_out, ffn_norm=ffn_norm, w_router_group=w_router_group,
             b_router_group=b_router_group, w_router_expert=w_router_expert,
             b_router_expert=b_router_expert, w_expert_gate=w_expert_gate,
             w_expert_up=w_expert_up, w_expert_down=w_expert_down)
    x2 = x.reshape(B * S, D).astype(F32)
    for l in range(mix_norm.shape[0]):
        x2 = _layer(x2, p, l, batch=B, seq=S)
    return x2.reshape(B, S, D).astype(x.dtype)
```
